```python
import math
import jax
import jax.numpy as jnp
from jax import lax
import numpy as np

D_MODEL = 1024
BATCH = 4
SEQ = 4096
DEPTH = 4
DEC_BATCH = 128
DEC_SEQ = 4
PAST_LEN = 2048
PAGE_SIZE = 128

N_MIXERS = 4
PLE_DIM = 256
D_FF = 4 * D_MODEL
EPS = 1e-6
CHUNK = 128
D_GATE = 2 * D_MODEL
N_GROUPS_A = 8
CONV_WIDTH = 31
HEAD_DIM = 64
N_HEADS_C = D_MODEL // HEAD_DIM
N_HEADS_D = D_MODEL // HEAD_DIM
Q_BLOCK = 128
ATTN_SCALE = HEAD_DIM ** -0.5
SB_LOGIT_OFFSET = 7.0
SB_QK_WEIGHT_SCALE = 0.5
DSW_PATTERNS = ((128, 1), (512, 4), (2048, 16))
N_GROUPS_D = len(DSW_PATTERNS)

kernel_name = 'hybrid_gmlp_conformer_stickbreak_dilated_decode_step'


def rms_norm(x, g):
    xf = x.astype(jnp.float32)
    y = xf * lax.rsqrt(jnp.mean(xf * xf, axis=-1, keepdims=True) + EPS)
    return (y * g.astype(jnp.float32)).astype(x.dtype)


def layer_norm(x, g, b):
    xf = x.astype(jnp.float32)
    mu = jnp.mean(xf, axis=-1, keepdims=True)
    xc = xf - mu
    y = xc * lax.rsqrt(jnp.mean(xc * xc, axis=-1, keepdims=True) + EPS)
    return (y * g.astype(jnp.float32) + b.astype(jnp.float32)).astype(x.dtype)


def channel_and_embedding(x, p, g_ffn, w1, w2, g_ple, w_pin, w_pgate, b_pgate):
    x = x + jnp.square(jax.nn.relu(rms_norm(x, g_ffn) @ w1)) @ w2
    gate = jax.nn.sigmoid(rms_norm(x, g_ple) @ w_pgate + b_pgate)
    return x + gate * (p @ w_pin)


def chunk_gmlp(h, w_in, g_v, w_s, b_s, w_out):
    b, t, _ = h.shape
    u, v = jnp.split(jax.nn.gelu(h @ w_in), 2, axis=-1)
    vn = rms_norm(v, g_v)
    n_chunks = -(-t // CHUNK)
    t_pad = n_chunks * CHUNK
    vc = jnp.pad(vn, ((0, 0), (0, t_pad - t), (0, 0))).reshape(
        b, n_chunks, CHUNK, N_GROUPS_A, D_GATE // N_GROUPS_A)
    w_causal = jnp.where(jnp.tril(jnp.ones((CHUNK, CHUNK), dtype=bool)), w_s, 0.0)
    mixed = jnp.einsum('gts,bnsgc->bntgc', w_causal, vc) + b_s.T[:, :, None]
    mixed = mixed.reshape(b, t_pad, D_GATE)[:, :t]
    last = (t - 1) // CHUNK * CHUNK
    return (u * mixed) @ w_out, vn[:, last:]


def conformer_conv(h, past, w_in, w_dw, b_dw, g_ln, b_ln, w_out):
    a, gate = jnp.split(h @ w_in, 2, axis=-1)
    c = a * jax.nn.sigmoid(gate)
    if past is None:
        past = jnp.zeros((c.shape[0], CONV_WIDTH - 1, c.shape[2]), c.dtype)
    ext = jnp.concatenate([past.astype(c.dtype), c], axis=1)
    y = lax.conv_general_dilated(ext, w_dw[:, None, :].astype(c.dtype), (1,), 'VALID',
                                 dimension_numbers=('NWC', 'WIO', 'NWC'),
                                 feature_group_count=c.shape[2]) + b_dw
    y = jax.nn.silu(layer_norm(y, g_ln, b_ln))
    return y @ w_out, ext[:, -(CONV_WIDTH - 1):]


def stick_breaking_weights(z, q_pos, k_pos):
    causal = k_pos[None, :] < q_pos[:, None]
    log_keep = jnp.where(causal, jax.nn.log_sigmoid(-z), 0.0)
    log_after = lax.cumsum(log_keep, axis=z.ndim - 1, reverse=True) - log_keep
    return jnp.where(causal, jnp.exp(jax.nn.log_sigmoid(z) + log_after), 0.0)


def sb_qkv(h, w_qkv, b_q, b_k):
    b, t, _ = h.shape
    qkv = (h @ w_qkv).reshape(b, t, 3, N_HEADS_C, HEAD_DIM)
    return qkv[:, :, 0] + b_q, qkv[:, :, 1] + b_k, qkv[:, :, 2]


def sb_mixer_prompt(h, w_qkv, b_q, b_k, w_o):
    q, k, v = sb_qkv(h, w_qkv, b_q, b_k)
    b, t = h.shape[:2]
    nb = t // Q_BLOCK
    qb = q.reshape(b, nb, Q_BLOCK, N_HEADS_C, HEAD_DIM).swapaxes(0, 1)
    k_pos = jnp.arange(t)

    def block(args):
        q_blk, n = args
        z = jnp.einsum('bqhc,bkhc->bhqk', q_blk, k, preferred_element_type=jnp.float32) * ATTN_SCALE
        a = stick_breaking_weights(z, n * Q_BLOCK + jnp.arange(Q_BLOCK), k_pos)
        return jnp.einsum('bhqk,bkhc->bqhc', a.astype(v.dtype), v)

    o = lax.map(block, (qb, jnp.arange(nb))).swapaxes(0, 1).reshape(b, t, N_HEADS_C * HEAD_DIM)
    return o @ w_o, k, v


def sb_mixer_sample(h, cache_k, cache_v, page_table, w_qkv, b_q, b_k, w_o):
    q, k, v = sb_qkv(h, w_qkv, b_q, b_k)
    db, s = h.shape[:2]
    past = page_table.shape[1] * PAGE_SIZE
    k_past = cache_k[page_table].reshape(db, past, N_HEADS_C, HEAD_DIM)
    v_past = cache_v[page_table].reshape(db, past, N_HEADS_C, HEAD_DIM)
    z = jnp.concatenate([
        jnp.einsum('bqhc,bkhc->bhqk', q, k_past, preferred_element_type=jnp.float32),
        jnp.einsum('bqhc,bkhc->bhqk', q, k, preferred_element_type=jnp.float32)], axis=-1) * ATTN_SCALE
    a = stick_breaking_weights(z, past + jnp.arange(s), jnp.arange(past + s)).astype(v.dtype)
    o = (jnp.einsum('bhqk,bkhc->bqhc', a[..., :past], v_past)
         + jnp.einsum('bhqk,bkhc->bqhc', a[..., past:], v))
    return o.reshape(db, s, N_HEADS_C * HEAD_DIM) @ w_o, k, v


def dsw_project(h, w_qkv, g_q, g_k):
    b, t, _ = h.shape
    qkv = (h @ w_qkv).reshape(b, t, N_GROUPS_D, 3, N_HEADS_D, HEAD_DIM)
    q = rms_norm(qkv[:, :, :, 0], g_q[:, None, :])
    k = rms_norm(qkv[:, :, :, 1], g_k[:, None, :])
    return q, k, qkv[:, :, :, 2]


def dilated_prompt(q, k, v, dil, n_back):
    b, t, h, c = q.shape
    L = t // dil
    nb = -(-L // Q_BLOCK)
    lp = nb * Q_BLOCK

    def by_residue(a):
        return a.reshape(b, L, dil, h, c).swapaxes(1, 2)

    qr = jnp.pad(by_residue(q), ((0, 0), (0, 0), (0, lp - L), (0, 0), (0, 0))).reshape(
        b, dil, nb, Q_BLOCK, h, c)

    def key_blocks(a):
        ap = jnp.pad(by_residue(a), ((0, 0), (0, 0), (Q_BLOCK, lp - L), (0, 0), (0, 0))).reshape(
            b, dil, nb + 1, Q_BLOCK, h, c)
        return jnp.concatenate([ap[:, :, :-1], ap[:, :, 1:]], axis=3)

    kb, vb = key_blocks(k), key_blocks(v)
    s = jnp.einsum('brnqhc,brnkhc->brnhqk', qr, kb, preferred_element_type=jnp.float32) * ATTN_SCALE
    qi = jnp.arange(Q_BLOCK)[:, None]
    kj = jnp.arange(2 * Q_BLOCK)[None, :]
    delta = qi + Q_BLOCK - kj
    key_m = jnp.arange(nb)[:, None, None] * Q_BLOCK - Q_BLOCK + kj[None]
    mask = (delta >= 0) & (delta <= n_back) & (key_m >= 0)
    s = jnp.where(mask[:, None], s, -jnp.inf)
    lse = jax.nn.logsumexp(s, axis=-1)
    p = jnp.exp(s - lse[..., None])
    o = jnp.einsum('brnhqk,brnkhc->brnqhc', p.astype(v.dtype), vb)
    o = o.reshape(b, dil, lp, h, c)[:, :, :L].swapaxes(1, 2).reshape(b, t, h, c)
    lse = lse.swapaxes(3, 4).reshape(b, dil, lp, h)[:, :, :L].swapaxes(1, 2).reshape(b, t, h)
    return o, lse


def dilated_sample(q, k, v, k_buf, v_buf, dil, n_back):
    db, s, h, c = q.shape
    lb = k_buf.shape[1]
    idx = lb + jnp.arange(s)[:, None] - dil * jnp.arange(n_back + 1)[None, :]
    valid = idx >= 0
    in_buf = (idx < lb)[None, :, :, None, None]
    i_buf = jnp.clip(idx, 0, lb - 1)
    i_new = jnp.clip(idx - lb, 0, s - 1)
    kg = jnp.where(in_buf, k_buf[:, i_buf], k[:, i_new])
    vg = jnp.where(in_buf, v_buf[:, i_buf], v[:, i_new])
    sc = jnp.einsum('bshc,bsnhc->bhsn', q, kg, preferred_element_type=jnp.float32) * ATTN_SCALE
    sc = jnp.where(valid, sc, -jnp.inf)
    lse = jax.nn.logsumexp(sc, axis=-1)
    p = jnp.exp(sc - lse[..., None])
    o = jnp.einsum('bhsn,bsnhc->bshc', p.astype(v.dtype), vg)
    return o, lse.swapaxes(1, 2)


def merge_by_denominator(outs, lses, w_o):
    w = jax.nn.softmax(jnp.stack(lses), axis=0)
    o = jnp.sum(w[..., None] * jnp.stack(outs).astype(jnp.float32), axis=0)
    b, t = o.shape[:2]
    return o.reshape(b, t, N_HEADS_D * HEAD_DIM).astype(outs[0].dtype) @ w_o


def dsw_mixer_prompt(h, w_qkv, g_q, g_k, w_o):
    q, k, v = dsw_project(h, w_qkv, g_q, g_k)
    t = h.shape[1]
    outs, lses, rows = [], [], []
    for g, (win, dil) in enumerate(DSW_PATTERNS):
        o, l = dilated_prompt(q[:, :, g], k[:, :, g], v[:, :, g], dil, win // dil)
        outs.append(o)
        lses.append(l)
        keep = min(win, t)
        rows += [k[:, t - keep:, g], v[:, t - keep:, g]]
    return merge_by_denominator(outs, lses, w_o), rows


def dsw_mixer_sample(h, buffers, w_qkv, g_q, g_k, w_o):
    q, k, v = dsw_project(h, w_qkv, g_q, g_k)
    outs, lses, rows = [], [], []
    for g, (win, dil) in enumerate(DSW_PATTERNS):
        o, l = dilated_sample(q[:, :, g], k[:, :, g], v[:, :, g],
                              buffers[2 * g], buffers[2 * g + 1], dil, win // dil)
        outs.append(o)
        lses.append(l)
        rows += [k[:, :, g], v[:, :, g]]
    return merge_by_denominator(outs, lses, w_o), rows


def setup_inputs(seed: int = 0) -> dict:
    key = jax.random.key(seed)
    ks = iter(jax.random.split(key, 48))

    def nrm(shape, scale=1.0):
        return jax.random.normal(next(ks), shape, jnp.float32) * scale

    def gain(shape):
        return 1.0 + nrm(shape, 0.01)

    n_pages = PAST_LEN // PAGE_SIZE
    n_used = DEC_BATCH * n_pages
    n_phys = n_used + max(1, n_used // 4)
    d_att = N_HEADS_D * HEAD_DIM
    d_c = N_HEADS_C * HEAD_DIM
    w0, w1, w2 = [min(w, PAST_LEN) for (w, _) in DSW_PATTERNS]
    sb_dir = nrm((N_HEADS_C, HEAD_DIM))
    sb_dir = sb_dir / jnp.linalg.norm(sb_dir, axis=-1, keepdims=True)
    sb_mag = math.sqrt(SB_LOGIT_OFFSET * math.sqrt(HEAD_DIM))
    b_c_q = sb_mag * sb_dir + nrm((N_HEADS_C, HEAD_DIM), 0.05)
    b_c_k = -sb_mag * sb_dir + nrm((N_HEADS_C, HEAD_DIM), 0.05)
    qkv_cols = jnp.array([SB_QK_WEIGHT_SCALE, SB_QK_WEIGHT_SCALE, 1.0], jnp.float32)[:, None]
    w_c_qkv = (nrm((D_MODEL, 3, d_c), D_MODEL ** -0.5) * qkv_cols).reshape(D_MODEL, 3 * d_c)
    cache_sb_k = b_c_k + nrm((n_phys, PAGE_SIZE, N_HEADS_C, HEAD_DIM), SB_QK_WEIGHT_SCALE)
    return {
        'x_prompt': nrm((BATCH, SEQ, D_MODEL)),
        'x_sample': nrm((DEC_BATCH, DEC_SEQ, D_MODEL)),
        'cache_sb_k': cache_sb_k,
        'cache_sb_v': nrm((n_phys, PAGE_SIZE, N_HEADS_C, HEAD_DIM)),
        'cache_dsw0_k': nrm((DEC_BATCH, w0, N_HEADS_D, HEAD_DIM)),
        'cache_dsw0_v': nrm((DEC_BATCH, w0, N_HEADS_D, HEAD_DIM)),
        'cache_dsw1_k': nrm((DEC_BATCH, w1, N_HEADS_D, HEAD_DIM)),
        'cache_dsw1_v': nrm((DEC_BATCH, w1, N_HEADS_D, HEAD_DIM)),
        'cache_dsw2_k': nrm((DEC_BATCH, w2, N_HEADS_D, HEAD_DIM)),
        'cache_dsw2_v': nrm((DEC_BATCH, w2, N_HEADS_D, HEAD_DIM)),
        'state_conv': nrm((DEC_BATCH, CONV_WIDTH - 1, D_MODEL), 0.5),
        'page_table': jax.random.permutation(next(ks), n_phys)[:n_used].reshape(
            DEC_BATCH, n_pages).astype(jnp.int32),
        'p_prompt': nrm((DEPTH, BATCH, SEQ, PLE_DIM)),
        'p_sample': nrm((DEPTH, DEC_BATCH, DEC_SEQ, PLE_DIM)),
        'g_mix': gain((DEPTH, D_MODEL)),
        'g_ffn': gain((DEPTH, D_MODEL)),
        'g_ple': gain((DEPTH, D_MODEL)),
        'w_ff1': nrm((DEPTH, D_MODEL, D_FF), D_MODEL ** -0.5),
        'w_ff2': nrm((DEPTH, D_FF, D_MODEL), D_FF ** -0.5),
        'w_ple_in': nrm((DEPTH, PLE_DIM, D_MODEL), PLE_DIM ** -0.5),
        'w_ple_gate': nrm((DEPTH, D_MODEL, D_MODEL), D_MODEL ** -0.5),
        'b_ple_gate': nrm((DEPTH, D_MODEL), 0.02),
        'w_a_in': nrm((D_MODEL, 2 * D_GATE), D_MODEL ** -0.5),
        'g_a_v': gain((D_GATE,)),
        'w_a_s': nrm((N_GROUPS_A, CHUNK, CHUNK), CHUNK ** -0.5),
        'b_a_s': gain((N_GROUPS_A, CHUNK)),
        'w_a_out': nrm((D_GATE, D_MODEL), D_GATE ** -0.5),
        'w_b_in': nrm((D_MODEL, 2 * D_MODEL), D_MODEL ** -0.5),
        'w_b_dw': nrm((CONV_WIDTH, D_MODEL), CONV_WIDTH ** -0.5),
        'b_b_dw': nrm((D_MODEL,), 0.02),
        'g_b_ln': gain((D_MODEL,)),
        'b_b_ln': nrm((D_MODEL,), 0.02),
        'w_b_out': nrm((D_MODEL, D_MODEL), D_MODEL ** -0.5),
        'w_c_qkv': w_c_qkv,
        'b_c_q': b_c_q,
        'b_c_k': b_c_k,
        'w_c_o': nrm((d_c, D_MODEL), d_c ** -0.5),
        'w_d_qkv': nrm((D_MODEL, N_GROUPS_D * 3 * d_att), D_MODEL ** -0.5),
        'g_d_q': gain((N_GROUPS_D, HEAD_DIM)),
        'g_d_k': gain((N_GROUPS_D, HEAD_DIM)),
        'w_d_o': nrm((d_att, D_MODEL), d_att ** -0.5),
    }


def reference(x_prompt, x_sample, cache_sb_k, cache_sb_v, cache_dsw0_k, cache_dsw0_v,
              cache_dsw1_k, cache_dsw1_v, cache_dsw2_k, cache_dsw2_v, state_conv, page_table,
              p_prompt, p_sample, g_mix, g_ffn, g_ple, w_ff1, w_ff2, w_ple_in, w_ple_gate,
              b_ple_gate, w_a_in, g_a_v, w_a_s, b_a_s, w_a_out, w_b_in, w_b_dw, b_b_dw, g_b_ln,
              b_b_ln, w_b_out, w_c_qkv, b_c_q, b_c_k, w_c_o, w_d_qkv, g_d_q, g_d_k, w_d_o):
    xp, xs = x_prompt, x_sample
    dsw_buffers = (cache_dsw0_k, cache_dsw0_v, cache_dsw1_k, cache_dsw1_v, cache_dsw2_k, cache_dsw2_v)
    for i in range(DEPTH):
        kind = i % N_MIXERS
        hp = rms_norm(xp, g_mix[i])
        hs = rms_norm(xs, g_mix[i])
        if kind == 0:
            mp, chunk_v_p = chunk_gmlp(hp, w_a_in, g_a_v, w_a_s, b_a_s, w_a_out)
            ms, chunk_v_s = chunk_gmlp(hs, w_a_in, g_a_v, w_a_s, b_a_s, w_a_out)
        elif kind == 1:
            mp, conv_p = conformer_conv(hp, None, w_b_in, w_b_dw, b_b_dw, g_b_ln, b_b_ln, w_b_out)
            ms, conv_s = conformer_conv(hs, state_conv, w_b_in, w_b_dw, b_b_dw, g_b_ln, b_b_ln, w_b_out)
        elif kind == 2:
            mp, sb_k_p, sb_v_p = sb_mixer_prompt(hp, w_c_qkv, b_c_q, b_c_k, w_c_o)
            ms, sb_k_s, sb_v_s = sb_mixer_sample(hs, cache_sb_k, cache_sb_v, page_table,
                                                 w_c_qkv, b_c_q, b_c_k, w_c_o)
        else:
            mp, (dk0_p, dv0_p, dk1_p, dv1_p, dk2_p, dv2_p) = dsw_mixer_prompt(
                hp, w_d_qkv, g_d_q, g_d_k, w_d_o)
            ms, (dk0_s, dv0_s, dk1_s, dv1_s, dk2_s, dv2_s) = dsw_mixer_sample(
                hs, dsw_buffers, w_d_qkv, g_d_q, g_d_k, w_d_o)
        xp = channel_and_embedding(xp + mp, p_prompt[i], g_ffn[i], w_ff1[i], w_ff2[i],
                                   g_ple[i], w_ple_in[i], w_ple_gate[i], b_ple_gate[i])
        xs = channel_and_embedding(xs + ms, p_sample[i], g_ffn[i], w_ff1[i], w_ff2[i],
                                   g_ple[i], w_ple_in[i], w_ple_gate[i], b_ple_gate[i])
    y_prompt, y_sample = xp, xs
    return (y_prompt, y_sample, chunk_v_p, chunk_v_s, conv_p, conv_s,
            sb_k_p, sb_v_p, sb_k_s, sb_v_s,
            dk0_p, dv0_p, dk1_p, dv1_p, dk2_p, dv2_p,
            dk0_s, dv0_s, dk1_s, dv1_s, dk2_s, dv2_s)
```

```python
import functools
import math

import jax
import jax.numpy as jnp
from jax import lax
from jax.experimental import pallas as pl
from jax.experimental.pallas import tpu as pltpu

F32 = jnp.float32
BF16 = jnp.bfloat16

D_MODEL = 1024
D_FF = 4 * D_MODEL
PLE_DIM = 256
EPS = 1e-6
CHUNK = 128
D_GATE = 2 * D_MODEL
N_GROUPS_A = 8
CONV_WIDTH = 31
CONV_PAST = CONV_WIDTH - 1
HEAD_DIM = 64
N_HEADS = D_MODEL // HEAD_DIM
Q_BLOCK = 128
PAGE_SIZE = 128
ATTN_SCALE = HEAD_DIM ** -0.5
DSW_PATTERNS = ((128, 1), (512, 4), (2048, 16))
N_GROUPS_D = len(DSW_PATTERNS)
NEG_BIG = -1e30
LANES = 128

V7X_VMEM_BYTES = 64 * 1024 * 1024
VMEM_LIMIT = V7X_VMEM_BYTES - 8 * 1024 * 1024
ROW_TILE = 512
CONV_HALO = 32
CONV_ROWS = 32
SB_PAGES_PER_STEP = 4
DSW_KEY_CHUNK = 512


def _params(*sem):
    return pltpu.CompilerParams(dimension_semantics=sem, vmem_limit_bytes=VMEM_LIMIT)


def _const_spec(shape):
    nd = len(shape)
    return pl.BlockSpec(shape, lambda *_: (0,) * nd, pipeline_mode=pl.Buffered(1))


def _dot(a, b):
    return jnp.dot(a, b, preferred_element_type=F32)


def _dot_nt(a, b):
    return lax.dot_general(a, b, (((1,), (1,)), ((), ())), preferred_element_type=F32)


def _rms(x, g):
    return x * lax.rsqrt(jnp.mean(x * x, axis=-1, keepdims=True) + EPS) * g


def _sigmoid(x):
    return 1.0 / (1.0 + jnp.exp(-x))


def _gelu(x):
    return 0.5 * x * (1.0 + jnp.tanh(math.sqrt(2.0 / math.pi) * (x + 0.044715 * (x * x * x))))


def _lane_tile(a, reps):
    return a if reps == 1 else jnp.concatenate([a] * reps, axis=1)


def _ffn_ple_body(x_ref, p_ref, gf_ref, w1_ref, w2_ref, gp_ref, wpi_ref, wpg_ref, bpg_ref,
                  o_ref, xn_ref, acc_ref):
    j = pl.program_id(1)

    @pl.when(j == 0)
    def _():
        x = x_ref[...]
        xn_ref[...] = _rms(x, gf_ref[...]).astype(BF16)
        acc_ref[...] = x

    h = jnp.square(jnp.maximum(_dot(xn_ref[...], w1_ref[...]), 0.0))
    acc_ref[...] += _dot(h.astype(BF16), w2_ref[...])

    @pl.when(j == pl.num_programs(1) - 1)
    def _():
        x2 = acc_ref[...]
        gate = _sigmoid(_dot(_rms(x2, gp_ref[...]).astype(BF16), wpg_ref[...]) + bpg_ref[...])
        o_ref[...] = x2 + gate * _dot(p_ref[...].astype(BF16), wpi_ref[...])


def _ffn_ple(x, p, layer, g_ffn, w1b, w2b, g_ple, wpib, wpgb, b_pg, *, tm, tf):
    m = x.shape[0]
    grid = (m // tm, D_FF // tf)
    row = lambda i, j: (i, 0)
    vec = pl.BlockSpec((1, D_MODEL), lambda i, j: (0, 0))
    return pl.pallas_call(
        _ffn_ple_body,
        grid=grid,
        in_specs=[
            pl.BlockSpec((tm, D_MODEL), row),
            pl.BlockSpec((None, tm, PLE_DIM), lambda i, j: (layer, i, 0)),
            vec,
            pl.BlockSpec((None, D_MODEL, tf), lambda i, j: (layer, 0, j)),
            pl.BlockSpec((None, tf, D_MODEL), lambda i, j: (layer, j, 0)),
            vec,
            pl.BlockSpec((None, PLE_DIM, D_MODEL), lambda i, j: (layer, 0, 0)),
            pl.BlockSpec((None, D_MODEL, D_MODEL), lambda i, j: (layer, 0, 0)),
            vec,
        ],
        out_specs=pl.BlockSpec((tm, D_MODEL), row),
        out_shape=jax.ShapeDtypeStruct((m, D_MODEL), F32),
        scratch_shapes=[pltpu.VMEM((tm, D_MODEL), BF16), pltpu.VMEM((tm, D_MODEL), F32)],
        compiler_params=_params("parallel", "arbitrary"),
        name="ffn_ple",
    )(x, p, g_ffn[layer][None], w1b, w2b, g_ple[layer][None], wpib, wpgb, b_pg[layer][None])


def _mixer_a_body(x_ref, g_ref, win_ref, gv_ref, s_ref, bs_ref, wout_ref, o_ref, vn_ref, gated_ref):
    x = x_ref[...]
    h = _rms(x, g_ref[...]).astype(BF16)
    v = _gelu(_dot(h, win_ref[:, D_GATE:]))
    vn = v * lax.rsqrt(jnp.mean(v * v, axis=-1, keepdims=True) + EPS) * gv_ref[...]
    vn_ref[...] = vn
    cg = D_GATE // N_GROUPS_A
    for g in range(N_GROUPS_A):
        cols = slice(g * cg, (g + 1) * cg)
        u = _gelu(_dot(h, win_ref[:, cols]))
        mixed = _dot(s_ref[g], vn_ref[:, cols].astype(BF16)) + _lane_tile(bs_ref[g], cg // LANES)
        gated_ref[:, cols] = (u * mixed).astype(BF16)
    o_ref[...] = x + _dot(gated_ref[...], wout_ref[...])


def _mixer_a(x, g_mix, w_in_b, g_v, w_s, b_s, w_out_b, *, chunk, tm):
    m = x.shape[0]
    reps = tm // chunk
    pos = jnp.arange(tm)
    same = (pos[:, None] // chunk) == (pos[None, :] // chunk)
    causal = pos[None, :] <= pos[:, None]
    w_small = w_s[:, :chunk, :chunk]
    s_mat = jnp.where((same & causal)[None], jnp.tile(w_small, (1, reps, reps)), 0.0).astype(BF16)
    bias = jnp.broadcast_to(jnp.tile(b_s[:, :chunk], (1, reps))[:, :, None], (N_GROUPS_A, tm, LANES))
    row = lambda i: (i, 0)
    return pl.pallas_call(
        _mixer_a_body,
        grid=(m // tm,),
        in_specs=[
            pl.BlockSpec((tm, D_MODEL), row),
            _const_spec((1, D_MODEL)),
            _const_spec((D_MODEL, 2 * D_GATE)),
            _const_spec((1, D_GATE)),
            _const_spec((N_GROUPS_A, tm, tm)),
            _const_spec((N_GROUPS_A, tm, LANES)),
            _const_spec((D_GATE, D_MODEL)),
        ],
        out_specs=[pl.BlockSpec((tm, D_MODEL), row), pl.BlockSpec((tm, D_GATE), row)],
        out_shape=[jax.ShapeDtypeStruct((m, D_MODEL), F32), jax.ShapeDtypeStruct((m, D_GATE), F32)],
        scratch_shapes=[pltpu.VMEM((tm, D_GATE), BF16)],
        compiler_params=_params("parallel"),
        name="mixer_a",
    )(x, g_mix[None], w_in_b, g_v[None], s_mat, bias, w_out_b)


def _glu(h, win_ref):
    return _dot(h, win_ref[:, :D_MODEL]) * _sigmoid(_dot(h, win_ref[:, D_MODEL:]))


def _ln_silu(y, g, b):
    mu = jnp.mean(y, axis=-1, keepdims=True)
    yc = y - mu
    z = yc * lax.rsqrt(jnp.mean(yc * yc, axis=-1, keepdims=True) + EPS) * g + b
    return z * _sigmoid(z)


def _mixer_b_prompt_body(x_ref, g_ref, win_ref, wdw_ref, bdw_ref, gln_ref, bln_ref, wout_ref,
                         o_ref, conv_ref, ext_ref, act_ref, *, tm, tiles_per_seq):
    i = pl.program_id(0)
    x = x_ref[...]
    c = _glu(_rms(x, g_ref[...]).astype(BF16), win_ref)

    @pl.when(i % tiles_per_seq == 0)
    def _():
        ext_ref[0:CONV_HALO, :] = jnp.zeros((CONV_HALO, D_MODEL), F32)

    @pl.when(i % tiles_per_seq != 0)
    def _():
        ext_ref[0:CONV_HALO, :] = ext_ref[tm:tm + CONV_HALO, :]

    ext_ref[CONV_HALO:CONV_HALO + tm, :] = c
    conv_ref[...] = c[tm - CONV_PAST:, :]
    base = CONV_HALO - CONV_PAST
    for r0 in range(0, tm, CONV_ROWS):
        acc = jnp.broadcast_to(bdw_ref[...], (CONV_ROWS, D_MODEL))
        for j in range(CONV_WIDTH):
            acc = acc + wdw_ref[j:j + 1, :] * ext_ref[base + r0 + j:base + r0 + j + CONV_ROWS, :]
        act_ref[r0:r0 + CONV_ROWS, :] = _ln_silu(acc, gln_ref[...], bln_ref[...]).astype(BF16)
    o_ref[...] = x + _dot(act_ref[...], wout_ref[...])


def _mixer_b_prompt(x, g_mix, w_in_b, w_dw, b_dw, g_ln, b_ln, w_out_b, *, seq, tm):
    m = x.shape[0]
    nb = m // seq
    tiles_per_seq = seq // tm
    row = lambda i: (i, 0)
    vec = _const_spec((1, D_MODEL))
    return pl.pallas_call(
        functools.partial(_mixer_b_prompt_body, tm=tm, tiles_per_seq=tiles_per_seq),
        grid=(m // tm,),
        in_specs=[
            pl.BlockSpec((tm, D_MODEL), row),
            vec,
            _const_spec((D_MODEL, 2 * D_MODEL)),
            _const_spec((CONV_WIDTH, D_MODEL)),
            vec, vec, vec,
            _const_spec((D_MODEL, D_MODEL)),
        ],
        out_specs=[pl.BlockSpec((tm, D_MODEL), row),
                   pl.BlockSpec((None, CONV_PAST, D_MODEL), lambda i: (i // tiles_per_seq, 0, 0))],
        out_shape=[jax.ShapeDtypeStruct((m, D_MODEL), F32),
                   jax.ShapeDtypeStruct((nb, CONV_PAST, D_MODEL), F32)],
        scratch_shapes=[pltpu.VMEM((tm + CONV_HALO, D_MODEL), F32), pltpu.VMEM((tm, D_MODEL), BF16)],
        compiler_params=_params("arbitrary"),
        name="mixer_b_prompt",
    )(x, g_mix[None], w_in_b, w_dw, b_dw[None], g_ln[None], b_ln[None], w_out_b)


def _mixer_b_sample_body(x_ref, st_ref, g_ref, win_ref, wdw_ref, bdw_ref, gln_ref, bln_ref, wout_ref,
                         o_ref, c_ref, act_ref, *, nb, ns):
    x = x_ref[...]
    c_ref[...] = _glu(_rms(x, g_ref[...]).astype(BF16), win_ref)
    for s in range(ns):
        acc = jnp.broadcast_to(bdw_ref[...], (nb, D_MODEL))
        for j in range(CONV_WIDTH):
            k = s + j
            src = st_ref[k] if k < CONV_PAST else c_ref[(k - CONV_PAST) * nb:(k - CONV_PAST + 1) * nb, :]
            acc = acc + wdw_ref[j:j + 1, :] * src
        act_ref[s * nb:(s + 1) * nb, :] = _ln_silu(acc, gln_ref[...], bln_ref[...]).astype(BF16)
    o_ref[...] = x + _dot(act_ref[...], wout_ref[...])


def _mixer_b_sample(x_sb, state_t, g_mix, w_in_b, w_dw, b_dw, g_ln, b_ln, w_out_b, *, nb, ns):
    m = nb * ns
    vec = _const_spec((1, D_MODEL))
    full = _const_spec((m, D_MODEL))
    return pl.pallas_call(
        functools.partial(_mixer_b_sample_body, nb=nb, ns=ns),
        grid=(1,),
        in_specs=[full, _const_spec((CONV_PAST, nb, D_MODEL)), vec,
                  _const_spec((D_MODEL, 2 * D_MODEL)), _const_spec((CONV_WIDTH, D_MODEL)),
                  vec, vec, vec, _const_spec((D_MODEL, D_MODEL))],
        out_specs=[pl.BlockSpec((m, D_MODEL), lambda i: (0, 0)), pl.BlockSpec((m, D_MODEL), lambda i: (0, 0))],
        out_shape=[jax.ShapeDtypeStruct((m, D_MODEL), F32), jax.ShapeDtypeStruct((m, D_MODEL), F32)],
        scratch_shapes=[pltpu.VMEM((m, D_MODEL), BF16)],
        compiler_params=_params("arbitrary"),
        name="mixer_b_sample",
    )(x_sb, state_t, g_mix[None], w_in_b, w_dw, b_dw[None], g_ln[None], b_ln[None], w_out_b)


def _proj_res_body(x_ref, a_ref, w_ref, o_ref):
    o_ref[...] = x_ref[...] + _dot(a_ref[...].astype(BF16), w_ref[...])


def _proj_res(x, a, w_b, *, tm):
    m = x.shape[0]
    row = lambda i: (i, 0)
    return pl.pallas_call(
        _proj_res_body,
        grid=(m // tm,),
        in_specs=[pl.BlockSpec((tm, D_MODEL), row), pl.BlockSpec((tm, D_MODEL), row),
                  _const_spec((D_MODEL, D_MODEL))],
        out_specs=pl.BlockSpec((tm, D_MODEL), row),
        out_shape=jax.ShapeDtypeStruct((m, D_MODEL), F32),
        compiler_params=_params("parallel"),
        name="proj_res",
    )(x, a, w_b)


def _proj_rows_body(x_ref, g_ref, w_ref, bias_ref, gain_ref, hs_ref, *rest, has_bias, normed, scales):
    outs, h_ref = rest[:-1], rest[-1]

    @pl.when(pl.program_id(1) == 0)
    def _():
        h_ref[...] = _rms(x_ref[...], g_ref[...]).astype(BF16)

    h = h_ref[...]
    for j, out in enumerate(outs):
        y = _dot(h, w_ref[:, j * D_MODEL:(j + 1) * D_MODEL])
        if has_bias[j]:
            y = y + bias_ref[j]
        if normed[j]:
            y = y * lax.rsqrt(_dot((y * y).astype(BF16), hs_ref[...]) + EPS) * gain_ref[j]
        out[...] = (y * scales[j]).astype(out.dtype)


def _proj_rows(x, g_mix, w_b, bias, gain, *, n_groups, has_bias, normed, scales, dtypes, tm):
    m = x.shape[0]
    n_out = len(dtypes)
    lane_head = jnp.arange(D_MODEL) // HEAD_DIM
    head_mean = jnp.where(lane_head[:, None] == lane_head[None, :], 1.0 / HEAD_DIM, 0.0).astype(BF16)
    per_group = pl.BlockSpec((None, n_out, 1, D_MODEL), lambda i, g: (g, 0, 0, 0))
    out = pl.BlockSpec((None, tm, D_MODEL), lambda i, g: (g, i, 0))
    return pl.pallas_call(
        functools.partial(_proj_rows_body, has_bias=has_bias, normed=normed, scales=scales),
        grid=(m // tm, n_groups),
        in_specs=[pl.BlockSpec((tm, D_MODEL), lambda i, g: (i, 0)),
                  pl.BlockSpec((1, D_MODEL), lambda i, g: (0, 0)),
                  pl.BlockSpec((D_MODEL, n_out * D_MODEL), lambda i, g: (0, g)),
                  per_group, per_group,
                  pl.BlockSpec((D_MODEL, D_MODEL), lambda i, g: (0, 0))],
        out_specs=[out] * n_out,
        out_shape=[jax.ShapeDtypeStruct((n_groups, m, D_MODEL), dt) for dt in dtypes],
        scratch_shapes=[pltpu.VMEM((tm, D_MODEL), BF16)],
        compiler_params=_params("parallel", "arbitrary"),
        name="proj_rows",
    )(x, g_mix[None], w_b, bias, gain, head_mean)


def _store_cols(ref, yt):
    if len(ref.shape) == 2:
        ref[...] = yt.astype(ref.dtype)
    else:
        for c in range(ref.shape[0]):
            ref[c] = yt[:, c * LANES:(c + 1) * LANES].astype(ref.dtype)


def _proj_cols_body(x_ref, g_ref, wt_ref, bias_ref, gain_ref, *outs, n_out, has_bias, normed):
    h = _rms(x_ref[...], g_ref[...]).astype(BF16)
    reps = h.shape[0] // LANES
    for j in range(n_out):
        yt = _dot_nt(wt_ref[j], h)
        if has_bias[j]:
            yt = yt + _lane_tile(bias_ref[j], reps)
        if normed[j]:
            gain = _lane_tile(gain_ref[j], reps)
            heads = []
            for hh in range(N_HEADS):
                blk = yt[hh * HEAD_DIM:(hh + 1) * HEAD_DIM, :]
                heads.append(blk * lax.rsqrt(jnp.mean(blk * blk, axis=0, keepdims=True) + EPS) * gain)
            yt = jnp.concatenate(heads, axis=0)
        for out in outs[j::n_out]:
            _store_cols(out, yt)


def _proj_cols(x, g_mix, wt_b, bias_col, gain_col, *, has_bias, normed, tm, grid, x_map, out_specs, out_shape):
    n_out = wt_b.shape[0]
    return pl.pallas_call(
        functools.partial(_proj_cols_body, n_out=n_out, has_bias=has_bias, normed=normed),
        grid=grid,
        in_specs=[pl.BlockSpec((tm, D_MODEL), x_map),
                  _const_spec((1, D_MODEL)), _const_spec(wt_b.shape),
                  _const_spec(bias_col.shape), _const_spec(gain_col.shape)],
        out_specs=out_specs,
        out_shape=out_shape,
        compiler_params=_params(*(["parallel"] * len(grid))),
        name="proj_cols",
    )(x, g_mix[None], wt_b, bias_col, gain_col)


def _col(v):
    return jnp.broadcast_to(v[:, :, None], v.shape + (LANES,))


def _sb_block(z, carry, tri, valid):
    n = z.shape[1]
    log_keep = -(jnp.maximum(z, 0.0) + jnp.log(1.0 + jnp.exp(-jnp.abs(z))))
    if valid is not None:
        log_keep = jnp.where(valid, log_keep, 0.0)
    hi = log_keep.astype(BF16)
    lo = (log_keep - hi.astype(F32)).astype(BF16)
    sums = _dot(hi, tri) + _dot(lo, tri)
    a = jnp.exp(z + log_keep + sums[:, :n] + carry)
    if valid is not None:
        a = jnp.where(valid, a, 0.0)
    return a, carry + sums[:, n:]


def _sb_tri(n):
    j = jnp.arange(n)
    return jnp.concatenate([(j[:, None] > j[None, :]), jnp.ones((n, n), bool)], axis=1).astype(BF16)


def _sb_prompt_body(q_ref, kt_ref, vt_ref, tri_ref, o_ref):
    qi = pl.program_id(2)
    q = q_ref[...]
    tri = tri_ref[...]
    heads = q.shape[1] // HEAD_DIM
    row = lax.broadcasted_iota(jnp.int32, (Q_BLOCK, Q_BLOCK), 0)
    col = lax.broadcasted_iota(jnp.int32, (Q_BLOCK, Q_BLOCK), 1)
    strictly_before = col < row

    def sweep(kb, state, valid):
        kblk = kt_ref[kb]
        vblk = vt_ref[kb]
        new = []
        for hh in range(heads):
            o, carry = state[hh]
            sl = slice(hh * HEAD_DIM, (hh + 1) * HEAD_DIM)
            a, carry = _sb_block(_dot(q[:, sl], kblk[sl, :]), carry, tri, valid)
            new.append((o + _dot_nt(a.astype(BF16), vblk[sl, :]), carry))
        return tuple(new)

    zero = tuple((jnp.zeros((Q_BLOCK, HEAD_DIM), F32), jnp.zeros((Q_BLOCK, Q_BLOCK), F32))
                 for _ in range(heads))
    state = sweep(qi, zero, strictly_before)
    state = lax.fori_loop(0, qi, lambda it, st: sweep(qi - 1 - it, st, None), state)
    o_ref[...] = jnp.concatenate([o for o, _ in state], axis=1)


def _sb_prompt(qs_b, kt_b, vt_b, *, nb, seq):
    lanes = 2 * HEAD_DIM
    nq = seq // Q_BLOCK
    kv = pl.BlockSpec((nq, lanes, Q_BLOCK), lambda b, hp, qi: (b, hp, 0))
    qo = lambda b, hp, qi: (b * nq + qi, hp)
    return pl.pallas_call(
        _sb_prompt_body,
        grid=(nb, D_MODEL // lanes, nq),
        in_specs=[pl.BlockSpec((Q_BLOCK, lanes), qo), kv, kv, _const_spec((Q_BLOCK, 2 * Q_BLOCK))],
        out_specs=pl.BlockSpec((Q_BLOCK, lanes), qo),
        out_shape=jax.ShapeDtypeStruct((nb * seq, D_MODEL), F32),
        compiler_params=_params("parallel", "parallel", "arbitrary"),
        name="sb_prompt",
    )(qs_b, kt_b, vt_b, _sb_tri(Q_BLOCK))


def _head_rows(q):
    ns = q.shape[0]
    rows = jnp.concatenate([jnp.broadcast_to(q[s:s + 1, :], (N_HEADS, D_MODEL)) for s in range(ns)], axis=0)
    r = lax.broadcasted_iota(jnp.int32, rows.shape, 0)
    lane = lax.broadcasted_iota(jnp.int32, rows.shape, 1)
    return jnp.where((lane >> 6) == (r & (N_HEADS - 1)), rows, 0.0).astype(BF16)


def _own_head(full, ns):
    r = lax.broadcasted_iota(jnp.int32, (N_HEADS, D_MODEL), 0)
    lane = lax.broadcasted_iota(jnp.int32, (N_HEADS, D_MODEL), 1)
    own = (lane >> 6) == r
    return [jnp.sum(jnp.where(own, full[s * N_HEADS:(s + 1) * N_HEADS, :], 0.0), axis=0, keepdims=True)
            for s in range(ns)]


def _new_rows(slab_ref, new_ref, ns):
    slab_ref[...] = jnp.zeros(slab_ref.shape, F32)
    slab_ref[0:ns, :] = new_ref[...]


def _sb_sample_body(pt_ref, q_ref, kn_ref, vn_ref, tri_ref, *rest, ns, pages):
    kt_refs, vt_refs = rest[:pages], rest[pages:2 * pages]
    o_ref, qh_ref, acc_ref, carry_ref, ks_ref, vs_ref = rest[2 * pages:]
    c = pl.program_id(1)
    rows = ns * N_HEADS
    tri = tri_ref[...]

    def sweep(z, valid):
        a, carry = _sb_block(z, carry_ref[...], tri, valid)
        carry_ref[...] = carry
        return a.astype(BF16)

    @pl.when(c == 0)
    def _():
        qh_ref[...] = _head_rows(q_ref[...])
        acc_ref[...] = jnp.zeros(acc_ref.shape, F32)
        carry_ref[...] = jnp.zeros(carry_ref.shape, F32)
        _new_rows(ks_ref, kn_ref, ns)
        _new_rows(vs_ref, vn_ref, ns)
        r = lax.broadcasted_iota(jnp.int32, (rows, PAGE_SIZE), 0)
        col = lax.broadcasted_iota(jnp.int32, (rows, PAGE_SIZE), 1)
        a = sweep(_dot_nt(qh_ref[...], ks_ref[...].astype(BF16)), col < (r >> 4))
        acc_ref[...] += _dot(a, vs_ref[...].astype(BF16))

    for p in range(pages):
        a = sweep(_dot(qh_ref[...], kt_refs[p][...].astype(BF16)), None)
        acc_ref[...] += _dot_nt(a, vt_refs[p][...].astype(BF16))

    @pl.when(c == pl.num_programs(1) - 1)
    def _():
        out = _own_head(acc_ref[...], ns)
        for s in range(ns):
            o_ref[s:s + 1, :] = out[s]


def _sb_sample(q_s, k_new, v_new, cache_kt, cache_vt, page_table, *, nb, ns):
    n_pages = page_table.shape[1]
    pages = SB_PAGES_PER_STEP
    steps = n_pages // pages
    rows = ns * N_HEADS
    tok = pl.BlockSpec((None, ns, D_MODEL), lambda b, c, pt: (b, 0, 0))

    def page_spec(p):
        return pl.BlockSpec((None, D_MODEL, PAGE_SIZE),
                            lambda b, c, pt: (pt[b * n_pages + n_pages - 1 - (c * pages + p)], 0, 0))

    grid_spec = pltpu.PrefetchScalarGridSpec(
        num_scalar_prefetch=1,
        grid=(nb, steps),
        in_specs=[tok, tok, tok, pl.BlockSpec((PAGE_SIZE, 2 * PAGE_SIZE), lambda b, c, pt: (0, 0))]
        + [page_spec(p) for p in range(pages)] * 2,
        out_specs=tok,
        scratch_shapes=[pltpu.VMEM((rows, D_MODEL), BF16), pltpu.VMEM((rows, D_MODEL), F32),
                        pltpu.VMEM((rows, PAGE_SIZE), F32),
                        pltpu.VMEM((PAGE_SIZE, D_MODEL), F32), pltpu.VMEM((PAGE_SIZE, D_MODEL), F32)],
    )
    return pl.pallas_call(
        functools.partial(_sb_sample_body, ns=ns, pages=pages),
        grid_spec=grid_spec,
        out_shape=jax.ShapeDtypeStruct((nb, ns, D_MODEL), F32),
        compiler_params=_params("parallel", "arbitrary"),
        name="sb_sample",
    )(page_table.reshape(-1), q_s, k_new, v_new, _sb_tri(PAGE_SIZE), *([cache_kt] * pages), *([cache_vt] * pages))


def _dsw_prompt_body(q_ref, kp_ref, kc_ref, vp_ref, vc_ref, o_ref, l_ref):
    n = pl.program_id(2)
    row = lax.broadcasted_iota(jnp.int32, (Q_BLOCK, Q_BLOCK), 0)
    col = lax.broadcasted_iota(jnp.int32, (Q_BLOCK, Q_BLOCK), 1)
    bias_prev = jnp.where((col >= row) & (n > 0), 0.0, NEG_BIG)
    bias_cur = jnp.where(col <= row, 0.0, NEG_BIG)
    lanes = 2 * HEAD_DIM
    for hp in range(D_MODEL // lanes):
        cols = slice(hp * lanes, (hp + 1) * lanes)
        q2, kp2, kc2, vp2, vc2 = q_ref[:, cols], kp_ref[:, cols], kc_ref[:, cols], vp_ref[:, cols], vc_ref[:, cols]
        outs, lses = [], []
        for hh in range(2):
            sl = slice(hh * HEAD_DIM, (hh + 1) * HEAD_DIM)
            s_prev = _dot_nt(q2[:, sl], kp2[:, sl]) + bias_prev
            s_cur = _dot_nt(q2[:, sl], kc2[:, sl]) + bias_cur
            mx = jnp.maximum(jnp.max(s_prev, axis=-1, keepdims=True), jnp.max(s_cur, axis=-1, keepdims=True))
            p_prev = jnp.exp(s_prev - mx)
            p_cur = jnp.exp(s_cur - mx)
            den = jnp.sum(p_prev, axis=-1, keepdims=True) + jnp.sum(p_cur, axis=-1, keepdims=True)
            o = _dot(p_prev.astype(BF16), vp2[:, sl]) + _dot(p_cur.astype(BF16), vc2[:, sl])
            outs.append(o / den)
            lses.append(jnp.broadcast_to(mx + jnp.log(den), (Q_BLOCK, HEAD_DIM)))
        o_ref[:, cols] = jnp.concatenate(outs, axis=1)
        l_ref[:, cols] = jnp.concatenate(lses, axis=1)


def _dsw_prompt(qs_b, k_b, v_b, g, dil, *, nb, seq):
    length = seq // dil
    nblk = length // Q_BLOCK
    view = lambda a: a.reshape(N_GROUPS_D, nb, length, dil * D_MODEL)
    cur = pl.BlockSpec((None, None, Q_BLOCK, D_MODEL), lambda b, r, n: (g, b, n, r))
    prev = pl.BlockSpec((None, None, Q_BLOCK, D_MODEL), lambda b, r, n: (g, b, jnp.maximum(n - 1, 0), r))
    out = pl.BlockSpec((None, Q_BLOCK, D_MODEL), lambda b, r, n: (b, n, r))
    sds = jax.ShapeDtypeStruct((nb, length, dil * D_MODEL), F32)
    o, lse = pl.pallas_call(
        _dsw_prompt_body,
        grid=(nb, dil, nblk),
        in_specs=[cur, prev, cur, prev, cur],
        out_specs=[out, out],
        out_shape=[sds, sds],
        compiler_params=_params("parallel", "parallel", "arbitrary"),
        name="dsw_prompt",
    )(view(qs_b), view(k_b), view(k_b), view(v_b), view(v_b))
    return o.reshape(nb * seq, D_MODEL), lse.reshape(nb * seq, D_MODEL)


def _merge_proj_body(x_ref, o0_ref, o1_ref, o2_ref, l0_ref, l1_ref, l2_ref, w_ref, out_ref):
    l0, l1, l2 = l0_ref[...], l1_ref[...], l2_ref[...]
    mx = jnp.maximum(jnp.maximum(l0, l1), l2)
    e0, e1, e2 = jnp.exp(l0 - mx), jnp.exp(l1 - mx), jnp.exp(l2 - mx)
    o = (e0 * o0_ref[...] + e1 * o1_ref[...] + e2 * o2_ref[...]) / (e0 + e1 + e2)
    out_ref[...] = x_ref[...] + _dot(o.astype(BF16), w_ref[...])


def _merge_proj(x, outs, lses, w_b, *, tm):
    m = x.shape[0]
    blk = pl.BlockSpec((tm, D_MODEL), lambda i: (i, 0))
    return pl.pallas_call(
        _merge_proj_body,
        grid=(m // tm,),
        in_specs=[blk] * 7 + [_const_spec((D_MODEL, D_MODEL))],
        out_specs=blk,
        out_shape=jax.ShapeDtypeStruct((m, D_MODEL), F32),
        compiler_params=_params("parallel"),
        name="merge_proj",
    )(x, *outs, *lses, w_b)


def _softmax_accumulate(m_ref, l_ref, acc_ref, s, value_dot):
    m_old = m_ref[...]
    m_new = jnp.maximum(m_old, jnp.max(s, axis=-1, keepdims=True))
    alpha = jnp.exp(m_old - m_new)
    p = jnp.exp(s - m_new[:, 0:1])
    l_ref[...] = alpha * l_ref[...] + jnp.sum(p, axis=-1, keepdims=True)
    acc_ref[...] = alpha[:, 0:1] * acc_ref[...] + value_dot(p.astype(BF16))
    m_ref[...] = m_new


def _dsw_sample_body(q_ref, kn_ref, vn_ref, k0_ref, v0_ref, k1_ref, v1_ref, k2_ref, v2_ref, o_ref,
                     qh_ref, m_ref, l_ref, acc_ref, ks_ref, vs_ref, *, ns):
    c = pl.program_id(1)
    rows = ns * N_HEADS
    step = lax.broadcasted_iota(jnp.int32, (rows, PAGE_SIZE), 0) >> 4
    col = lax.broadcasted_iota(jnp.int32, (rows, PAGE_SIZE), 1)

    def attend_buffer(g, kt_ref, vt_ref, valid):
        s = jnp.where(valid, _dot(qh_ref[g], kt_ref[...].astype(BF16)), NEG_BIG)
        _softmax_accumulate(m_ref.at[g], l_ref.at[g], acc_ref.at[g], s,
                            lambda p: _dot_nt(p, vt_ref[...].astype(BF16)))

    def key_pos(n):
        return lax.broadcasted_iota(jnp.int32, (rows, n), 1)

    def key_step(n):
        return lax.broadcasted_iota(jnp.int32, (rows, n), 0) >> 4

    @pl.when(c == 0)
    def _():
        m_ref[...] = jnp.full(m_ref.shape, NEG_BIG, F32)
        l_ref[...] = jnp.zeros(l_ref.shape, F32)
        acc_ref[...] = jnp.zeros(acc_ref.shape, F32)
        for g, (_, dil) in enumerate(DSW_PATTERNS):
            qh_ref[g] = _head_rows(q_ref[g])
            _new_rows(ks_ref, kn_ref.at[g], ns)
            _new_rows(vs_ref, vn_ref.at[g], ns)
            new_ok = (col <= step) if dil == 1 else (col == step)
            s = jnp.where(new_ok, _dot_nt(qh_ref[g], ks_ref[...].astype(BF16)), NEG_BIG)
            _softmax_accumulate(m_ref.at[g], l_ref.at[g], acc_ref.at[g], s,
                                lambda p: _dot(p, vs_ref[...].astype(BF16)))
        n0, n1 = k0_ref.shape[1], k1_ref.shape[1]
        attend_buffer(0, k0_ref, v0_ref, key_pos(n0) >= key_step(n0))
        attend_buffer(1, k1_ref, v1_ref, (key_pos(n1) & (DSW_PATTERNS[1][1] - 1)) == key_step(n1))

    n2 = k2_ref.shape[1]
    attend_buffer(2, k2_ref, v2_ref, (key_pos(n2) & (DSW_PATTERNS[2][1] - 1)) == key_step(n2))

    @pl.when(c == pl.num_programs(1) - 1)
    def _():
        lses = [m_ref[g][:, 0:1] + jnp.log(l_ref[g][:, 0:1]) for g in range(N_GROUPS_D)]
        mx = jnp.maximum(jnp.maximum(lses[0], lses[1]), lses[2])
        es = [jnp.exp(l - mx) for l in lses]
        merged = sum(es[g] * (acc_ref[g] / l_ref[g][:, 0:1]) for g in range(N_GROUPS_D)) / (es[0] + es[1] + es[2])
        out = _own_head(merged, ns)
        for s in range(ns):
            o_ref[s:s + 1, :] = out[s]


def _dsw_sample(q_s, k_new, v_new, buffers_t, *, nb, ns):
    rows = ns * N_HEADS
    tok = pl.BlockSpec((N_GROUPS_D, None, ns, D_MODEL), lambda b, c: (0, b, 0, 0))
    specs = []
    for g, (win, dil) in enumerate(DSW_PATTERNS):
        assert buffers_t[2 * g].shape[2] == win and win // dil == PAGE_SIZE and (dil == 1 or ns <= dil)
        assert dil & (dil - 1) == 0 and DSW_KEY_CHUNK % dil == 0
    w0, w1, w2 = (w for w, _ in DSW_PATTERNS)
    assert max(w0, w1) <= DSW_KEY_CHUNK and w2 % DSW_KEY_CHUNK == 0
    whole = lambda w: pl.BlockSpec((None, D_MODEL, w), lambda b, c: (b, 0, 0))
    chunk = pl.BlockSpec((None, D_MODEL, DSW_KEY_CHUNK), lambda b, c: (b, 0, c))
    specs = [whole(w0), whole(w0), whole(w1), whole(w1), chunk, chunk]
    state = lambda lanes: pltpu.VMEM((N_GROUPS_D, rows, lanes), F32)
    return pl.pallas_call(
        functools.partial(_dsw_sample_body, ns=ns),
        grid=(nb, w2 // DSW_KEY_CHUNK),
        in_specs=[tok, tok, tok] + specs,
        out_specs=pl.BlockSpec((None, ns, D_MODEL), lambda b, c: (b, 0, 0)),
        out_shape=jax.ShapeDtypeStruct((nb, ns, D_MODEL), F32),
        scratch_shapes=[pltpu.VMEM((N_GROUPS_D, rows, D_MODEL), BF16), state(LANES), state(LANES), state(D_MODEL),
                        pltpu.VMEM((PAGE_SIZE, D_MODEL), F32), pltpu.VMEM((PAGE_SIZE, D_MODEL), F32)],
        compiler_params=_params("parallel", "arbitrary"),
        name="dsw_sample",
    )(q_s, k_new, v_new, *buffers_t)


def _positions_minor(a):
    return a.transpose(0, 2, 3, 1).reshape(a.shape[0], D_MODEL, a.shape[1])


def _positions_major(at):
    return at.reshape(at.shape[0], N_HEADS, HEAD_DIM, at.shape[2]).transpose(0, 3, 1, 2)


def _steps_minor(at, nb):
    return at.reshape(at.shape[0], N_HEADS, HEAD_DIM, nb).transpose(3, 0, 1, 2)


def kernel(x_prompt, x_sample, cache_sb_k, cache_sb_v, cache_dsw0_k, cache_dsw0_v, cache_dsw1_k, cache_dsw1_v, cache_dsw2_k, cache_dsw2_v, state_conv, page_table, p_prompt, p_sample, g_mix, g_ffn, g_ple, w_ff1, w_ff2, w_ple_in, w_ple_gate, b_ple_gate, w_a_in, g_a_v, w_a_s, b_a_s, w_a_out, w_b_in, w_b_dw, b_b_dw, g_b_ln, b_b_ln, w_b_out, w_c_qkv, b_c_q, b_c_k, w_c_o, w_d_qkv, g_d_q, g_d_k, w_d_o):
    nb_p, seq, _ = x_prompt.shape
    nb_s, ns, _ = x_sample.shape
    depth = g_mix.shape[0]
    mp, ms = nb_p * seq, nb_s * ns
    tm = ROW_TILE
    assert mp % tm == 0 and ms == tm and seq % tm == 0 and nb_s == LANES

    bf = lambda w: w.astype(BF16)
    w1b, w2b, wpib, wpgb = bf(w_ff1), bf(w_ff2), bf(w_ple_in), bf(w_ple_gate)
    xp = x_prompt.reshape(mp, D_MODEL)
    xs = x_sample.reshape(ms, D_MODEL)
    pp = p_prompt.reshape(depth, mp, PLE_DIM)
    ps = p_sample.reshape(depth, ms, PLE_DIM)
    to_sb = lambda a: a.reshape(nb_s, ns, D_MODEL).swapaxes(0, 1).reshape(ms, D_MODEL)
    from_sb = lambda a: a.reshape(ns, nb_s, D_MODEL).swapaxes(0, 1).reshape(ms, D_MODEL)
    tile_blocks = tm // LANES

    wa_in, wa_out = bf(w_a_in), bf(w_a_out)
    xp, vn_p = _mixer_a(xp, g_mix[0], wa_in, g_a_v, w_a_s, b_a_s, wa_out, chunk=CHUNK, tm=tm)
    xs, vn_s = _mixer_a(xs, g_mix[0], wa_in, g_a_v, w_a_s, b_a_s, wa_out, chunk=ns, tm=tm)
    chunk_v_p = vn_p.reshape(nb_p, seq, D_GATE)[:, (seq - 1) // CHUNK * CHUNK:]
    chunk_v_s = vn_s.reshape(nb_s, ns, D_GATE)

    def channel(xp, xs, i):
        args = (g_ffn, w1b, w2b, g_ple, wpib, wpgb, b_ple_gate)
        return (_ffn_ple(xp, pp, i, *args, tm=tm, tf=512), _ffn_ple(xs, ps, i, *args, tm=tm, tf=512))

    xp, xs = channel(xp, xs, 0)

    wb_in, wb_out = bf(w_b_in), bf(w_b_out)
    conv_args = (g_mix[1], wb_in, w_b_dw, b_b_dw, g_b_ln, b_b_ln, wb_out)
    xp, conv_p = _mixer_b_prompt(xp, *conv_args, seq=seq, tm=tm)
    state_t = state_conv.swapaxes(0, 1)
    xs_sb, c_sb = _mixer_b_sample(to_sb(xs), state_t, *conv_args, nb=nb_s, ns=ns)
    xs = from_sb(xs_sb)
    conv_s = jnp.concatenate([state_t[ns:], c_sb.reshape(ns, nb_s, D_MODEL)], axis=0).swapaxes(0, 1)
    xp, xs = channel(xp, xs, 1)

    wc_o = bf(w_c_o)
    wc = w_c_qkv.reshape(D_MODEL, 3, D_MODEL)
    wc_kv_t = bf(wc[:, 1:].transpose(1, 2, 0))
    zeros = jnp.zeros((D_MODEL,), F32)
    ones_rows = jnp.ones((1, 3, 1, D_MODEL), F32)
    bias_rows = jnp.stack([b_c_q.reshape(-1), b_c_k.reshape(-1), zeros]).reshape(1, 3, 1, D_MODEL)
    bias_cols = _col(jnp.stack([b_c_k.reshape(-1), zeros]))
    gain_cols = jnp.ones((2, HEAD_DIM, LANES), F32)
    (qp,) = _proj_rows(xp, g_mix[2], bf(wc[:, 0]), bias_rows[:, :1], ones_rows[:, :1], n_groups=1,
                       has_bias=(True,), normed=(False,), scales=(ATTN_SCALE,), dtypes=(BF16,), tm=tm)
    flat = pl.BlockSpec((None, D_MODEL, tm), lambda i: (i // (seq // tm), 0, i % (seq // tm)))
    blocked = pl.BlockSpec((tile_blocks, D_MODEL, LANES), lambda i: (i, 0, 0))
    sb_kt_p, sb_vt_p, ktb, vtb = _proj_cols(
        xp, g_mix[2], wc_kv_t, bias_cols, gain_cols, has_bias=(True, False), normed=(False, False), tm=tm,
        grid=(mp // tm,), x_map=lambda i: (i, 0), out_specs=[flat, flat, blocked, blocked],
        out_shape=[jax.ShapeDtypeStruct((nb_p, D_MODEL, seq), F32)] * 2
        + [jax.ShapeDtypeStruct((mp // LANES, D_MODEL, LANES), BF16)] * 2)
    xp = _proj_res(xp, _sb_prompt(qp[0], ktb, vtb, nb=nb_p, seq=seq), wc_o, tm=tm)
    qs, ks_new, vs_new = _proj_rows(xs, g_mix[2], bf(w_c_qkv), bias_rows, ones_rows, n_groups=1,
                                    has_bias=(True, True, False), normed=(False,) * 3,
                                    scales=(ATTN_SCALE, 1.0, 1.0), dtypes=(F32,) * 3, tm=tm)
    step_blocks = pl.BlockSpec((ns, D_MODEL, LANES), lambda i: (0, 0, 0))
    sb_kt_s, sb_vt_s = _proj_cols(
        to_sb(xs), g_mix[2], wc_kv_t, bias_cols, gain_cols, has_bias=(True, False), normed=(False, False), tm=tm,
        grid=(1,), x_map=lambda i: (0, 0), out_specs=[step_blocks] * 2,
        out_shape=[jax.ShapeDtypeStruct((ns, D_MODEL, nb_s), F32)] * 2)
    tok = lambda a: a.reshape(nb_s, ns, D_MODEL)
    o_s = _sb_sample(tok(qs[0]), tok(ks_new[0]), tok(vs_new[0]), _positions_minor(cache_sb_k),
                     _positions_minor(cache_sb_v), page_table, nb=nb_s, ns=ns)
    xs = _proj_res(xs, o_s.reshape(ms, D_MODEL), wc_o, tm=tm)
    xp, xs = channel(xp, xs, 2)

    wd_qkv, wd_o = bf(w_d_qkv), bf(w_d_o)
    wd = w_d_qkv.reshape(D_MODEL, N_GROUPS_D, 3, D_MODEL)
    tile_heads = lambda g_: jnp.tile(g_, (1, N_HEADS))
    gain_rows = jnp.stack([tile_heads(g_d_q), tile_heads(g_d_k), jnp.ones((N_GROUPS_D, D_MODEL), F32)],
                          axis=1).reshape(N_GROUPS_D, 3, 1, D_MODEL)
    zero_rows = jnp.zeros((N_GROUPS_D, 3, 1, D_MODEL), F32)
    rows_args = dict(n_groups=N_GROUPS_D, has_bias=(False,) * 3, normed=(True, True, False),
                     scales=(ATTN_SCALE, 1.0, 1.0), tm=tm)
    qp, kpb, vpb = _proj_rows(xp, g_mix[3], wd_qkv, zero_rows, gain_rows, dtypes=(BF16,) * 3, **rows_args)
    outs, lses = zip(*[_dsw_prompt(qp, kpb, vpb, g, dil, nb=nb_p, seq=seq)
                       for g, (_, dil) in enumerate(DSW_PATTERNS)])
    xp_attn = _merge_proj(xp, outs, lses, wd_o, tm=tm)
    qs, ks_new, vs_new = _proj_rows(xs, g_mix[3], wd_qkv, zero_rows, gain_rows, dtypes=(F32,) * 3, **rows_args)
    xs_sb = to_sb(xs)
    rows_p, rows_s = [], []
    for g, (win, _) in enumerate(DSW_PATTERNS):
        wt = bf(wd[:, g, 1:].transpose(1, 2, 0))
        gain = jnp.stack([jnp.broadcast_to(g_d_k[g][:, None], (HEAD_DIM, LANES)), jnp.ones((HEAD_DIM, LANES), F32)])
        no_bias = jnp.zeros((2, D_MODEL, LANES), F32)
        keep = min(win, seq)
        tw = min(tm, keep)
        first = (seq - keep) // tw
        kt, vt = _proj_cols(
            xp, g_mix[3], wt, no_bias, gain, has_bias=(False, False), normed=(True, False), tm=tw,
            grid=(nb_p, keep // tw), x_map=lambda b, j, first=first, per=seq // tw: (b * per + first + j, 0),
            out_specs=[pl.BlockSpec((None, D_MODEL, tw), lambda b, j: (b, 0, j))] * 2,
            out_shape=[jax.ShapeDtypeStruct((nb_p, D_MODEL, keep), F32)] * 2)
        rows_p += [_positions_major(kt), _positions_major(vt)]
        kt, vt = _proj_cols(
            xs_sb, g_mix[3], wt, no_bias, gain, has_bias=(False, False), normed=(True, False), tm=tm,
            grid=(1,), x_map=lambda i: (0, 0), out_specs=[step_blocks] * 2,
            out_shape=[jax.ShapeDtypeStruct((ns, D_MODEL, nb_s), F32)] * 2)
        rows_s += [_steps_minor(kt, nb_s), _steps_minor(vt, nb_s)]
    tok3 = lambda a: a.reshape(N_GROUPS_D, nb_s, ns, D_MODEL)
    buffers_t = [_positions_minor(b) for b in
                 (cache_dsw0_k, cache_dsw0_v, cache_dsw1_k, cache_dsw1_v, cache_dsw2_k, cache_dsw2_v)]
    o_s = _dsw_sample(tok3(qs), tok3(ks_new), tok3(vs_new), buffers_t, nb=nb_s, ns=ns)
    xs = _proj_res(xs, o_s.reshape(ms, D_MODEL), wd_o, tm=tm)
    xp, xs = channel(xp_attn, xs, 3)

    return (xp.reshape(nb_p, seq, D_MODEL), xs.reshape(nb_s, ns, D_MODEL), chunk_v_p, chunk_v_s,
            conv_p, conv_s,
            _positions_major(sb_kt_p), _positions_major(sb_vt_p),
            _steps_minor(sb_kt_s, nb_s), _steps_minor(sb_vt_s, nb_s),
            *rows_p, *rows_s)
```

```python
import functools
import math

import jax
import jax.numpy as jnp
from jax import lax
from jax.experimental import pallas as pl
from jax.experimental.pallas import tpu as pltpu

F32 = jnp.float32
BF16 = jnp.bfloat16

D_MODEL = 1024
D_FF = 4 * D_MODEL
PLE_DIM = 256
EPS = 1e-6
CHUNK = 128
D_GATE = 2 * D_MODEL
N_GROUPS_A = 8
CONV_WIDTH = 31
CONV_PAST = CONV_WIDTH - 1
HEAD_DIM = 64
N_HEADS = D_MODEL // HEAD_DIM
Q_BLOCK = 128
PAGE_SIZE = 128
ATTN_SCALE = HEAD_DIM ** -0.5
DSW_PATTERNS = ((128, 1), (512, 4), (2048, 16))
N_GROUPS_D = len(DSW_PATTERNS)
NEG_BIG = -1e30
LANES = 128

V7X_VMEM_BYTES = 64 * 1024 * 1024
VMEM_LIMIT = V7X_VMEM_BYTES - 8 * 1024 * 1024
ROW_TILE = 512
FFN_ROW_TILE = 1024
FFN_COL_TILE = 512
CONV_HALO = 32
CONV_ROWS = 32
SB_BLOCK = 256
SB_PAGES_PER_STEP = 4
LOG2E = math.log2(math.e)
DSW_KEY_CHUNK = 512


def _params(*sem):
    return pltpu.CompilerParams(dimension_semantics=sem, vmem_limit_bytes=VMEM_LIMIT)


def _const_spec(shape):
    nd = len(shape)
    return pl.BlockSpec(shape, lambda *_: (0,) * nd, pipeline_mode=pl.Buffered(1))


def _dot(a, b):
    return jnp.dot(a, b, preferred_element_type=F32)


def _dot_nt(a, b):
    return lax.dot_general(a, b, (((1,), (1,)), ((), ())), preferred_element_type=F32)


def _rms(x, g):
    return x * lax.rsqrt(jnp.mean(x * x, axis=-1, keepdims=True) + EPS) * g


def _sigmoid(x):
    return 1.0 / (1.0 + jnp.exp(-x))


def _gelu(x):
    return 0.5 * x * (1.0 + jnp.tanh(math.sqrt(2.0 / math.pi) * (x + 0.044715 * (x * x * x))))


def _lane_tile(a, reps):
    return a if reps == 1 else jnp.concatenate([a] * reps, axis=1)


def _ffn_ple_body(x_ref, p_ref, gf_ref, w1_ref, w2_ref, gp_ref, wpi_ref, wpg_ref, bpg_ref,
                  o_ref, xn_ref, acc_ref):
    j = pl.program_id(1)

    @pl.when(j == 0)
    def _():
        x = x_ref[...]
        xn_ref[...] = _rms(x, gf_ref[...]).astype(BF16)
        acc_ref[...] = x

    h = jnp.square(jnp.maximum(_dot(xn_ref[...], w1_ref[...]), 0.0))
    acc_ref[...] += _dot(h.astype(BF16), w2_ref[...])

    @pl.when(j == pl.num_programs(1) - 1)
    def _():
        x2 = acc_ref[...]
        gate = _sigmoid(_dot(_rms(x2, gp_ref[...]).astype(BF16), wpg_ref[...]) + bpg_ref[...])
        o_ref[...] = x2 + gate * _dot(p_ref[...].astype(BF16), wpi_ref[...])


def _ffn_ple(x, p, layer, g_ffn, w1b, w2b, g_ple, wpib, wpgb, b_pg, *, tm, tf):
    m = x.shape[0]
    grid = (m // tm, D_FF // tf)
    row = lambda i, j: (i, 0)
    vec = pl.BlockSpec((1, D_MODEL), lambda i, j: (0, 0))
    return pl.pallas_call(
        _ffn_ple_body,
        grid=grid,
        in_specs=[
            pl.BlockSpec((tm, D_MODEL), row),
            pl.BlockSpec((None, tm, PLE_DIM), lambda i, j: (layer, i, 0)),
            vec,
            pl.BlockSpec((None, D_MODEL, tf), lambda i, j: (layer, 0, j)),
            pl.BlockSpec((None, tf, D_MODEL), lambda i, j: (layer, j, 0)),
            vec,
            pl.BlockSpec((None, PLE_DIM, D_MODEL), lambda i, j: (layer, 0, 0)),
            pl.BlockSpec((None, D_MODEL, D_MODEL), lambda i, j: (layer, 0, 0)),
            vec,
        ],
        out_specs=pl.BlockSpec((tm, D_MODEL), row),
        out_shape=jax.ShapeDtypeStruct((m, D_MODEL), F32),
        scratch_shapes=[pltpu.VMEM((tm, D_MODEL), BF16), pltpu.VMEM((tm, D_MODEL), F32)],
        compiler_params=_params("parallel", "arbitrary"),
        name="ffn_ple",
    )(x, p, g_ffn[layer][None], w1b, w2b, g_ple[layer][None], wpib, wpgb, b_pg[layer][None])


def _mixer_a_body(x_ref, g_ref, win_ref, gv_ref, s_ref, bs_ref, wout_ref, o_ref, vn_ref, gated_ref):
    x = x_ref[...]
    h = _rms(x, g_ref[...]).astype(BF16)
    v = _gelu(_dot(h, win_ref[:, D_GATE:]))
    vn = v * lax.rsqrt(jnp.mean(v * v, axis=-1, keepdims=True) + EPS) * gv_ref[...]
    vn_ref[...] = vn
    cg = D_GATE // N_GROUPS_A
    for g in range(N_GROUPS_A):
        cols = slice(g * cg, (g + 1) * cg)
        u = _gelu(_dot(h, win_ref[:, cols]))
        mixed = _dot(s_ref[g], vn_ref[:, cols].astype(BF16)) + _lane_tile(bs_ref[g], cg // LANES)
        gated_ref[:, cols] = (u * mixed).astype(BF16)
    o_ref[...] = x + _dot(gated_ref[...], wout_ref[...])


def _mixer_a(x, g_mix, w_in_b, g_v, w_s, b_s, w_out_b, *, chunk, tm):
    m = x.shape[0]
    reps = tm // chunk
    pos = jnp.arange(tm)
    same = (pos[:, None] // chunk) == (pos[None, :] // chunk)
    causal = pos[None, :] <= pos[:, None]
    w_small = w_s[:, :chunk, :chunk]
    s_mat = jnp.where((same & causal)[None], jnp.tile(w_small, (1, reps, reps)), 0.0).astype(BF16)
    bias = jnp.broadcast_to(jnp.tile(b_s[:, :chunk], (1, reps))[:, :, None], (N_GROUPS_A, tm, LANES))
    row = lambda i: (i, 0)
    return pl.pallas_call(
        _mixer_a_body,
        grid=(m // tm,),
        in_specs=[
            pl.BlockSpec((tm, D_MODEL), row),
            _const_spec((1, D_MODEL)),
            _const_spec((D_MODEL, 2 * D_GATE)),
            _const_spec((1, D_GATE)),
            _const_spec((N_GROUPS_A, tm, tm)),
            _const_spec((N_GROUPS_A, tm, LANES)),
            _const_spec((D_GATE, D_MODEL)),
        ],
        out_specs=[pl.BlockSpec((tm, D_MODEL), row), pl.BlockSpec((tm, D_GATE), row)],
        out_shape=[jax.ShapeDtypeStruct((m, D_MODEL), F32), jax.ShapeDtypeStruct((m, D_GATE), F32)],
        scratch_shapes=[pltpu.VMEM((tm, D_GATE), BF16)],
        compiler_params=_params("parallel"),
        name="mixer_a",
    )(x, g_mix[None], w_in_b, g_v[None], s_mat, bias, w_out_b)


def _glu(h, win_ref):
    return _dot(h, win_ref[:, :D_MODEL]) * _sigmoid(_dot(h, win_ref[:, D_MODEL:]))


def _ln_silu(y, g, b):
    mu = jnp.mean(y, axis=-1, keepdims=True)
    yc = y - mu
    z = yc * lax.rsqrt(jnp.mean(yc * yc, axis=-1, keepdims=True) + EPS) * g + b
    return z * _sigmoid(z)


def _mixer_b_prompt_body(x_ref, g_ref, win_ref, wdw_ref, bdw_ref, gln_ref, bln_ref, wout_ref,
                         o_ref, conv_ref, ext_ref, act_ref, *, tm, tiles_per_seq):
    i = pl.program_id(0)
    x = x_ref[...]
    c = _glu(_rms(x, g_ref[...]).astype(BF16), win_ref)

    @pl.when(i % tiles_per_seq == 0)
    def _():
        ext_ref[0:CONV_HALO, :] = jnp.zeros((CONV_HALO, D_MODEL), F32)

    @pl.when(i % tiles_per_seq != 0)
    def _():
        ext_ref[0:CONV_HALO, :] = ext_ref[tm:tm + CONV_HALO, :]

    ext_ref[CONV_HALO:CONV_HALO + tm, :] = c
    conv_ref[...] = c[tm - CONV_PAST:, :]
    base = CONV_HALO - CONV_PAST
    for r0 in range(0, tm, CONV_ROWS):
        acc = jnp.broadcast_to(bdw_ref[...], (CONV_ROWS, D_MODEL))
        for j in range(CONV_WIDTH):
            acc = acc + wdw_ref[j:j + 1, :] * ext_ref[base + r0 + j:base + r0 + j + CONV_ROWS, :]
        act_ref[r0:r0 + CONV_ROWS, :] = _ln_silu(acc, gln_ref[...], bln_ref[...]).astype(BF16)
    o_ref[...] = x + _dot(act_ref[...], wout_ref[...])


def _mixer_b_prompt(x, g_mix, w_in_b, w_dw, b_dw, g_ln, b_ln, w_out_b, *, seq, tm):
    m = x.shape[0]
    nb = m // seq
    tiles_per_seq = seq // tm
    row = lambda i: (i, 0)
    vec = _const_spec((1, D_MODEL))
    return pl.pallas_call(
        functools.partial(_mixer_b_prompt_body, tm=tm, tiles_per_seq=tiles_per_seq),
        grid=(m // tm,),
        in_specs=[
            pl.BlockSpec((tm, D_MODEL), row),
            vec,
            _const_spec((D_MODEL, 2 * D_MODEL)),
            _const_spec((CONV_WIDTH, D_MODEL)),
            vec, vec, vec,
            _const_spec((D_MODEL, D_MODEL)),
        ],
        out_specs=[pl.BlockSpec((tm, D_MODEL), row),
                   pl.BlockSpec((None, CONV_PAST, D_MODEL), lambda i: (i // tiles_per_seq, 0, 0))],
        out_shape=[jax.ShapeDtypeStruct((m, D_MODEL), F32),
                   jax.ShapeDtypeStruct((nb, CONV_PAST, D_MODEL), F32)],
        scratch_shapes=[pltpu.VMEM((tm + CONV_HALO, D_MODEL), F32), pltpu.VMEM((tm, D_MODEL), BF16)],
        compiler_params=_params("arbitrary"),
        name="mixer_b_prompt",
    )(x, g_mix[None], w_in_b, w_dw, b_dw[None], g_ln[None], b_ln[None], w_out_b)


def _mixer_b_sample_body(x_ref, st_ref, g_ref, win_ref, wdw_ref, bdw_ref, gln_ref, bln_ref, wout_ref,
                         o_ref, c_ref, act_ref, *, nb, ns):
    x = x_ref[...]
    c_ref[...] = _glu(_rms(x, g_ref[...]).astype(BF16), win_ref)
    for s in range(ns):
        acc = jnp.broadcast_to(bdw_ref[...], (nb, D_MODEL))
        for j in range(CONV_WIDTH):
            k = s + j
            src = st_ref[k] if k < CONV_PAST else c_ref[(k - CONV_PAST) * nb:(k - CONV_PAST + 1) * nb, :]
            acc = acc + wdw_ref[j:j + 1, :] * src
        act_ref[s * nb:(s + 1) * nb, :] = _ln_silu(acc, gln_ref[...], bln_ref[...]).astype(BF16)
    o_ref[...] = x + _dot(act_ref[...], wout_ref[...])


def _mixer_b_sample(x_sb, state_t, g_mix, w_in_b, w_dw, b_dw, g_ln, b_ln, w_out_b, *, nb, ns):
    m = nb * ns
    vec = _const_spec((1, D_MODEL))
    full = _const_spec((m, D_MODEL))
    return pl.pallas_call(
        functools.partial(_mixer_b_sample_body, nb=nb, ns=ns),
        grid=(1,),
        in_specs=[full, _const_spec((CONV_PAST, nb, D_MODEL)), vec,
                  _const_spec((D_MODEL, 2 * D_MODEL)), _const_spec((CONV_WIDTH, D_MODEL)),
                  vec, vec, vec, _const_spec((D_MODEL, D_MODEL))],
        out_specs=[pl.BlockSpec((m, D_MODEL), lambda i: (0, 0)), pl.BlockSpec((m, D_MODEL), lambda i: (0, 0))],
        out_shape=[jax.ShapeDtypeStruct((m, D_MODEL), F32), jax.ShapeDtypeStruct((m, D_MODEL), F32)],
        scratch_shapes=[pltpu.VMEM((m, D_MODEL), BF16)],
        compiler_params=_params("arbitrary"),
        name="mixer_b_sample",
    )(x_sb, state_t, g_mix[None], w_in_b, w_dw, b_dw[None], g_ln[None], b_ln[None], w_out_b)


def _proj_res_body(x_ref, a_ref, w_ref, o_ref):
    o_ref[...] = x_ref[...] + _dot(a_ref[...].astype(BF16), w_ref[...])


def _proj_res(x, a, w_b, *, tm):
    m = x.shape[0]
    row = lambda i: (i, 0)
    return pl.pallas_call(
        _proj_res_body,
        grid=(m // tm,),
        in_specs=[pl.BlockSpec((tm, D_MODEL), row), pl.BlockSpec((tm, D_MODEL), row),
                  _const_spec((D_MODEL, D_MODEL))],
        out_specs=pl.BlockSpec((tm, D_MODEL), row),
        out_shape=jax.ShapeDtypeStruct((m, D_MODEL), F32),
        compiler_params=_params("parallel"),
        name="proj_res",
    )(x, a, w_b)


def _proj_rows_body(x_ref, g_ref, w_ref, bias_ref, gain_ref, hs_ref, *rest, has_bias, normed, scales):
    outs, h_ref = rest[:-1], rest[-1]

    @pl.when(pl.program_id(1) == 0)
    def _():
        h_ref[...] = _rms(x_ref[...], g_ref[...]).astype(BF16)

    h = h_ref[...]
    for j, out in enumerate(outs):
        y = _dot(h, w_ref[:, j * D_MODEL:(j + 1) * D_MODEL])
        if has_bias[j]:
            y = y + bias_ref[j]
        if normed[j]:
            y = y * lax.rsqrt(_dot((y * y).astype(BF16), hs_ref[...]) + EPS) * gain_ref[j]
        out[...] = (y * scales[j]).astype(out.dtype)


def _proj_rows(x, g_mix, w_b, bias, gain, *, n_groups, has_bias, normed, scales, dtypes, tm):
    m = x.shape[0]
    n_out = len(dtypes)
    lane_head = jnp.arange(D_MODEL) // HEAD_DIM
    head_mean = jnp.where(lane_head[:, None] == lane_head[None, :], 1.0 / HEAD_DIM, 0.0).astype(BF16)
    per_group = pl.BlockSpec((None, n_out, 1, D_MODEL), lambda i, g: (g, 0, 0, 0))
    out = pl.BlockSpec((None, tm, D_MODEL), lambda i, g: (g, i, 0))
    return pl.pallas_call(
        functools.partial(_proj_rows_body, has_bias=has_bias, normed=normed, scales=scales),
        grid=(m // tm, n_groups),
        in_specs=[pl.BlockSpec((tm, D_MODEL), lambda i, g: (i, 0)),
                  pl.BlockSpec((1, D_MODEL), lambda i, g: (0, 0)),
                  pl.BlockSpec((D_MODEL, n_out * D_MODEL), lambda i, g: (0, g)),
                  per_group, per_group,
                  pl.BlockSpec((D_MODEL, D_MODEL), lambda i, g: (0, 0))],
        out_specs=[out] * n_out,
        out_shape=[jax.ShapeDtypeStruct((n_groups, m, D_MODEL), dt) for dt in dtypes],
        scratch_shapes=[pltpu.VMEM((tm, D_MODEL), BF16)],
        compiler_params=_params("parallel", "arbitrary"),
        name="proj_rows",
    )(x, g_mix[None], w_b, bias, gain, head_mean)


def _store_cols(ref, yt):
    if len(ref.shape) == 2:
        ref[...] = yt.astype(ref.dtype)
    else:
        width = ref.shape[2]
        for c in range(ref.shape[0]):
            ref[c] = yt[:, c * width:(c + 1) * width].astype(ref.dtype)


def _proj_cols_body(x_ref, g_ref, wt_ref, bias_ref, gain_ref, *outs, n_out, has_bias, normed):
    h = _rms(x_ref[...], g_ref[...]).astype(BF16)
    reps = h.shape[0] // LANES
    for j in range(n_out):
        yt = _dot_nt(wt_ref[j], h)
        if has_bias[j]:
            yt = yt + _lane_tile(bias_ref[j], reps)
        if normed[j]:
            gain = _lane_tile(gain_ref[j], reps)
            heads = []
            for hh in range(N_HEADS):
                blk = yt[hh * HEAD_DIM:(hh + 1) * HEAD_DIM, :]
                heads.append(blk * lax.rsqrt(jnp.mean(blk * blk, axis=0, keepdims=True) + EPS) * gain)
            yt = jnp.concatenate(heads, axis=0)
        for out in outs[j::n_out]:
            _store_cols(out, yt)


def _proj_cols(x, g_mix, wt_b, bias_col, gain_col, *, has_bias, normed, tm, grid, x_map, out_specs, out_shape):
    n_out = wt_b.shape[0]
    return pl.pallas_call(
        functools.partial(_proj_cols_body, n_out=n_out, has_bias=has_bias, normed=normed),
        grid=grid,
        in_specs=[pl.BlockSpec((tm, D_MODEL), x_map),
                  _const_spec((1, D_MODEL)), _const_spec(wt_b.shape),
                  _const_spec(bias_col.shape), _const_spec(gain_col.shape)],
        out_specs=out_specs,
        out_shape=out_shape,
        compiler_params=_params(*(["parallel"] * len(grid))),
        name="proj_cols",
    )(x, g_mix[None], wt_b, bias_col, gain_col)


def _col(v):
    return jnp.broadcast_to(v[:, :, None], v.shape + (LANES,))


def _sb_block(z, tri, valid):
    n = z.shape[1]
    log_keep = -(jnp.maximum(z, 0.0) + jnp.log(1.0 + jnp.exp(-jnp.abs(z))))
    if valid is not None:
        log_keep = jnp.where(valid, log_keep, 0.0)
    hi = log_keep.astype(BF16)
    lo = (log_keep - hi.astype(F32)).astype(BF16)
    sums = _dot(hi, tri) + _dot(lo, tri)
    return z + log_keep + sums[:, :n], sums[:, n:]


def _sb_tri(n):
    j = jnp.arange(n)
    return jnp.concatenate([(j[:, None] > j[None, :]), jnp.ones((n, n), bool)], axis=1).astype(BF16)


def _sb_prompt_body(q_ref, kt_ref, vt_ref, tri_ref, o_ref, acc_ref, carry_ref):
    qb = pl.program_id(2)
    heads = q_ref.shape[1] // HEAD_DIM
    tri = tri_ref[...]
    acc_ref[...] = jnp.zeros(acc_ref.shape, F32)
    carry_ref[...] = jnp.zeros(carry_ref.shape, F32)
    row = lax.broadcasted_iota(jnp.int32, (SB_BLOCK, SB_BLOCK), 0)
    col = lax.broadcasted_iota(jnp.int32, (SB_BLOCK, SB_BLOCK), 1)
    strictly_before = col < row

    def sweep(kbs, valid):
        outs, carries = [], []
        for hh in range(heads):
            sl = slice(hh * HEAD_DIM, (hh + 1) * HEAD_DIM)
            q = q_ref[:, sl]
            parts = []
            for kb in kbs:
                z = _dot(q, kt_ref[kb, sl, :])
                sp = jnp.maximum(z, 0.0) + jnp.log2(1.0 + jnp.exp2(-jnp.abs(z)))
                if valid is not None:
                    sp = jnp.where(valid, sp, 0.0)
                hi = sp.astype(BF16)
                lo = (sp - hi.astype(F32)).astype(BF16)
                after = _dot(hi, tri) + _dot(lo, tri)
                parts.append((kb, z - sp - after, jnp.sum(sp, axis=-1, keepdims=True)))
            carry = carry_ref[hh]
            o = acc_ref[:, sl]
            for kb, t, total in parts:
                a = jnp.exp2(t - _lane_tile(carry, SB_BLOCK // LANES))
                if valid is not None:
                    a = jnp.where(valid, a, 0.0)
                o = o + _dot_nt(a.astype(BF16), vt_ref[kb, sl, :])
                carry = carry + total
            outs.append(o)
            carries.append(carry)
        acc_ref[...] = jnp.concatenate(outs, axis=1)
        carry_ref[...] = jnp.stack(carries)

    sweep([qb], strictly_before)

    @pl.when(qb % 2 == 1)
    def _():
        sweep([qb - 1], None)

    even = qb - qb % 2

    def pair(i, _):
        kb = even - 1 - 2 * i
        sweep([kb, kb - 1], None)
        return 0

    lax.fori_loop(0, qb // 2, pair, 0)
    o_ref[...] = acc_ref[...]


def _sb_prompt(qs_b, kt_b, vt_b, *, nb, seq):
    lanes = 2 * HEAD_DIM
    nq = seq // SB_BLOCK
    kv = pl.BlockSpec((nq, lanes, SB_BLOCK), lambda b, hp, qb: (b, hp, 0))
    qo = lambda b, hp, qb: (b * nq + qb, hp)
    j = jnp.arange(SB_BLOCK)
    tri = (j[:, None] > j[None, :]).astype(BF16)
    return pl.pallas_call(
        _sb_prompt_body,
        grid=(nb, D_MODEL // lanes, nq),
        in_specs=[pl.BlockSpec((SB_BLOCK, lanes), qo), kv, kv, _const_spec((SB_BLOCK, SB_BLOCK))],
        out_specs=pl.BlockSpec((SB_BLOCK, lanes), qo),
        out_shape=jax.ShapeDtypeStruct((nb * seq, D_MODEL), F32),
        scratch_shapes=[pltpu.VMEM((SB_BLOCK, lanes), F32), pltpu.VMEM((2, SB_BLOCK, LANES), F32)],
        compiler_params=_params("parallel", "parallel", "arbitrary"),
        name="sb_prompt",
    )(qs_b, kt_b, vt_b, tri)


def _head_rows(q):
    ns = q.shape[0]
    rows = jnp.concatenate([jnp.broadcast_to(q[s:s + 1, :], (N_HEADS, D_MODEL)) for s in range(ns)], axis=0)
    r = lax.broadcasted_iota(jnp.int32, rows.shape, 0)
    lane = lax.broadcasted_iota(jnp.int32, rows.shape, 1)
    return jnp.where((lane >> 6) == (r & (N_HEADS - 1)), rows, 0.0).astype(BF16)


def _own_head(full, ns):
    r = lax.broadcasted_iota(jnp.int32, (N_HEADS, D_MODEL), 0)
    lane = lax.broadcasted_iota(jnp.int32, (N_HEADS, D_MODEL), 1)
    own = (lane >> 6) == r
    return [jnp.sum(jnp.where(own, full[s * N_HEADS:(s + 1) * N_HEADS, :], 0.0), axis=0, keepdims=True)
            for s in range(ns)]


def _new_rows(slab_ref, new_ref, ns):
    slab_ref[...] = jnp.zeros(slab_ref.shape, F32)
    slab_ref[0:ns, :] = new_ref[...]


def _sb_sample_body(pt_ref, q_ref, kn_ref, vn_ref, tri_ref, *rest, ns, pages):
    kt_refs, vt_refs = rest[:pages], rest[pages:2 * pages]
    o_ref, qh_ref, acc_ref, carry_ref, ks_ref, vs_ref = rest[2 * pages:]
    c = pl.program_id(1)
    rows = ns * N_HEADS
    tri = tri_ref[...]

    @pl.when(c == 0)
    def _():
        qh_ref[...] = _head_rows(q_ref[...])
        _new_rows(ks_ref, kn_ref, ns)
        _new_rows(vs_ref, vn_ref, ns)
        r = lax.broadcasted_iota(jnp.int32, (rows, PAGE_SIZE), 0)
        col = lax.broadcasted_iota(jnp.int32, (rows, PAGE_SIZE), 1)
        valid = col < (r >> 4)
        t, total = _sb_block(_dot_nt(qh_ref[...], ks_ref[...].astype(BF16)), tri, valid)
        a = jnp.where(valid, jnp.exp(t), 0.0)
        acc_ref[...] = _dot(a.astype(BF16), vs_ref[...].astype(BF16))
        carry_ref[...] = total

    qh = qh_ref[...]
    fronts = [_sb_block(_dot(qh, kt_refs[p][...].astype(BF16)), tri, None) for p in range(pages)]
    carry = carry_ref[...]
    acc = acc_ref[...]
    for p, (t, total) in enumerate(fronts):
        acc = acc + _dot_nt(jnp.exp(t + carry).astype(BF16), vt_refs[p][...].astype(BF16))
        carry = carry + total
    acc_ref[...] = acc
    carry_ref[...] = carry

    @pl.when(c == pl.num_programs(1) - 1)
    def _():
        out = _own_head(acc_ref[...], ns)
        for s in range(ns):
            o_ref[s:s + 1, :] = out[s]


def _sb_sample(q_s, k_new, v_new, cache_kt, cache_vt, page_table, *, nb, ns):
    n_pages = page_table.shape[1]
    pages = SB_PAGES_PER_STEP
    steps = n_pages // pages
    rows = ns * N_HEADS
    tok = pl.BlockSpec((None, ns, D_MODEL), lambda b, c, pt: (b, 0, 0))

    def page_spec(p):
        return pl.BlockSpec((None, D_MODEL, PAGE_SIZE),
                            lambda b, c, pt: (pt[b * n_pages + n_pages - 1 - (c * pages + p)], 0, 0))

    grid_spec = pltpu.PrefetchScalarGridSpec(
        num_scalar_prefetch=1,
        grid=(nb, steps),
        in_specs=[tok, tok, tok, pl.BlockSpec((PAGE_SIZE, 2 * PAGE_SIZE), lambda b, c, pt: (0, 0))]
        + [page_spec(p) for p in range(pages)] * 2,
        out_specs=tok,
        scratch_shapes=[pltpu.VMEM((rows, D_MODEL), BF16), pltpu.VMEM((rows, D_MODEL), F32),
                        pltpu.VMEM((rows, PAGE_SIZE), F32),
                        pltpu.VMEM((PAGE_SIZE, D_MODEL), F32), pltpu.VMEM((PAGE_SIZE, D_MODEL), F32)],
    )
    return pl.pallas_call(
        functools.partial(_sb_sample_body, ns=ns, pages=pages),
        grid_spec=grid_spec,
        out_shape=jax.ShapeDtypeStruct((nb, ns, D_MODEL), F32),
        compiler_params=_params("parallel", "arbitrary"),
        name="sb_sample",
    )(page_table.reshape(-1), q_s, k_new, v_new, _sb_tri(PAGE_SIZE), *([cache_kt] * pages), *([cache_vt] * pages))


def _proj_rows_dsw_body(x_ref, g_ref, w_ref, gain_ref, hs_ref, *rest, dils):
    outs, y_ref = rest[:-1], rest[-1]
    h = _rms(x_ref[...], g_ref[...]).astype(BF16)
    tm = h.shape[0]
    for g, dil in enumerate(dils):
        for j in range(3):
            col = (3 * g + j) * D_MODEL
            y = _dot(h, w_ref[:, col:col + D_MODEL])
            if j < 2:
                y = y * lax.rsqrt(_dot((y * y).astype(BF16), hs_ref[...]) + EPS) * gain_ref[g, j]
            if j == 0:
                y = y * ATTN_SCALE
            out = outs[3 * g + j]
            if dil == 1:
                out[0] = y.astype(BF16)
            else:
                for c in range(D_MODEL // LANES):
                    y_ref[c] = y[:, c * LANES:(c + 1) * LANES]
                for r in range(dil):
                    rows = [y_ref[c, pl.ds(r, tm // dil, stride=dil), :] for c in range(D_MODEL // LANES)]
                    out[r] = jnp.concatenate(rows, axis=1).astype(BF16)


def _proj_rows_dsw(x, g_mix, w_b, gain, *, nb, seq, tm):
    m = x.shape[0]
    dils = tuple(d for _, d in DSW_PATTERNS)
    per_seq = seq // tm
    lane_head = jnp.arange(D_MODEL) // HEAD_DIM
    head_mean = jnp.where(lane_head[:, None] == lane_head[None, :], 1.0 / HEAD_DIM, 0.0).astype(BF16)
    out_specs, out_shape = [], []
    for dil in dils:
        for _ in range(3):
            out_specs.append(pl.BlockSpec((None, dil, tm // dil, D_MODEL),
                                          lambda i: (i // per_seq, 0, i % per_seq, 0)))
            out_shape.append(jax.ShapeDtypeStruct((nb, dil, seq // dil, D_MODEL), BF16))
    outs = pl.pallas_call(
        functools.partial(_proj_rows_dsw_body, dils=dils),
        grid=(m // tm,),
        in_specs=[pl.BlockSpec((tm, D_MODEL), lambda i: (i, 0)), _const_spec((1, D_MODEL)),
                  _const_spec(w_b.shape), _const_spec(gain.shape), _const_spec((D_MODEL, D_MODEL))],
        out_specs=out_specs,
        out_shape=out_shape,
        scratch_shapes=[pltpu.VMEM((D_MODEL // LANES, tm, LANES), F32)],
        compiler_params=_params("parallel"),
        name="proj_rows_dsw",
    )(x, g_mix[None], w_b, gain, head_mean)
    return [outs[3 * g:3 * g + 3] for g in range(len(dils))]


def _dsw_prompt_body(q_ref, kp_ref, kc_ref, vp_ref, vc_ref, o_ref, l_ref):
    n = pl.program_id(2)
    row = lax.broadcasted_iota(jnp.int32, (Q_BLOCK, Q_BLOCK), 0)
    col = lax.broadcasted_iota(jnp.int32, (Q_BLOCK, Q_BLOCK), 1)
    bias = jnp.concatenate([jnp.where((col >= row) & (n > 0), 0.0, NEG_BIG),
                            jnp.where(col <= row, 0.0, NEG_BIG)], axis=1)
    lanes = 2 * HEAD_DIM
    lane = lax.broadcasted_iota(jnp.int32, (1, lanes), 1)
    first = lane < HEAD_DIM
    keep = [first.astype(BF16), (~first).astype(BF16)]
    outs, lses = [], []
    for hp in range(D_MODEL // lanes):
        cols = slice(hp * lanes, (hp + 1) * lanes)
        q2 = q_ref[:, cols]
        k2 = jnp.concatenate([kp_ref[:, cols], kc_ref[:, cols]], axis=0)
        v2 = jnp.concatenate([vp_ref[:, cols], vc_ref[:, cols]], axis=0)
        o2, l2 = [], []
        for hh in range(2):
            s = _dot_nt(q2 * keep[hh], k2) + bias
            mx = jnp.max(s, axis=-1, keepdims=True)
            p = jnp.exp(s - mx)
            den = jnp.sum(p, axis=-1, keepdims=True)
            o2.append(_dot(p.astype(BF16), v2) / den)
            l2.append(mx + jnp.log(den))
        outs.append(jnp.where(first, o2[0], o2[1]))
        lses.append(jnp.where(first, l2[0], l2[1]))
    o_ref[...] = jnp.concatenate(outs, axis=1)
    l_ref[...] = jnp.concatenate(lses, axis=1)


def _dsw_prompt(qs_b, k_b, v_b, dil, *, nb, seq):
    length = seq // dil
    cur = pl.BlockSpec((None, None, Q_BLOCK, D_MODEL), lambda b, r, n: (b, r, n, 0))
    prev = pl.BlockSpec((None, None, Q_BLOCK, D_MODEL), lambda b, r, n: (b, r, jnp.maximum(n - 1, 0), 0))
    sds = jax.ShapeDtypeStruct((nb, dil, length, D_MODEL), F32)
    return pl.pallas_call(
        _dsw_prompt_body,
        grid=(nb, dil, length // Q_BLOCK),
        in_specs=[cur, prev, cur, prev, cur],
        out_specs=[cur, cur],
        out_shape=[sds, sds],
        compiler_params=_params("parallel", "parallel", "arbitrary"),
        name="dsw_prompt",
    )(qs_b, k_b, k_b, v_b, v_b)


def _merge_proj_body(x_ref, *rest, dils):
    n = len(dils)
    o_refs, l_refs, (w_ref, out_ref, tok_ref) = rest[:n], rest[n:2 * n], rest[2 * n:]
    tm = x_ref.shape[0]

    def token_major(ref, dil, slot):
        if dil == 1:
            return ref[0]
        for r in range(dil):
            blk = ref[r]
            for c in range(D_MODEL // LANES):
                tok_ref[slot, c, pl.ds(r, tm // dil, stride=dil), :] = blk[:, c * LANES:(c + 1) * LANES]
        return jnp.concatenate([tok_ref[slot, c] for c in range(D_MODEL // LANES)], axis=1)

    outs = [token_major(o_refs[g], dil, 2 * g) for g, dil in enumerate(dils)]
    lses = [token_major(l_refs[g], dil, 2 * g + 1) for g, dil in enumerate(dils)]
    mx = functools.reduce(jnp.maximum, lses)
    es = [jnp.exp(l - mx) for l in lses]
    o = sum(e * o for e, o in zip(es, outs)) / sum(es)
    out_ref[...] = x_ref[...] + _dot(o.astype(BF16), w_ref[...])


def _merge_proj(x, outs, lses, w_b, *, seq, tm):
    m = x.shape[0]
    dils = tuple(d for _, d in DSW_PATTERNS)
    per_seq = seq // tm
    row = pl.BlockSpec((tm, D_MODEL), lambda i: (i, 0))
    split = [pl.BlockSpec((None, dil, tm // dil, D_MODEL), lambda i: (i // per_seq, 0, i % per_seq, 0))
             for dil in dils]
    return pl.pallas_call(
        functools.partial(_merge_proj_body, dils=dils),
        grid=(m // tm,),
        in_specs=[row] + split + split + [_const_spec((D_MODEL, D_MODEL))],
        out_specs=row,
        out_shape=jax.ShapeDtypeStruct((m, D_MODEL), F32),
        scratch_shapes=[pltpu.VMEM((2 * len(dils), D_MODEL // LANES, tm, LANES), F32)],
        compiler_params=_params("parallel"),
        name="merge_proj",
    )(x, *outs, *lses, w_b)


def _softmax_accumulate(m_ref, l_ref, acc_ref, s, value_dot):
    m_old = m_ref[...]
    m_new = jnp.maximum(m_old, jnp.max(s, axis=-1, keepdims=True))
    alpha = jnp.exp(m_old - m_new)
    p = jnp.exp(s - m_new[:, 0:1])
    l_ref[...] = alpha * l_ref[...] + jnp.sum(p, axis=-1, keepdims=True)
    acc_ref[...] = alpha[:, 0:1] * acc_ref[...] + value_dot(p.astype(BF16))
    m_ref[...] = m_new


def _dsw_sample_body(q_ref, kn_ref, vn_ref, k0_ref, v0_ref, k1_ref, v1_ref, k2_ref, v2_ref, o_ref,
                     qh_ref, m_ref, l_ref, acc_ref, ks_ref, vs_ref, *, ns):
    c = pl.program_id(1)
    rows = ns * N_HEADS
    step = lax.broadcasted_iota(jnp.int32, (rows, PAGE_SIZE), 0) >> 4
    col = lax.broadcasted_iota(jnp.int32, (rows, PAGE_SIZE), 1)

    def buffer_scores(g, kt_ref):
        n, dil = kt_ref.shape[1], DSW_PATTERNS[g][1]
        pos = lax.broadcasted_iota(jnp.int32, (rows, n), 1)
        qstep = lax.broadcasted_iota(jnp.int32, (rows, n), 0) >> 4
        valid = (pos >= qstep) if dil == 1 else ((pos & (dil - 1)) == qstep)
        return jnp.where(valid, _dot(qh_ref[g], kt_ref[...].astype(BF16)), NEG_BIG)

    def row_max(s):
        return jnp.max(s, axis=-1, keepdims=True)

    def row_sum(p):
        return jnp.sum(p, axis=-1, keepdims=True)

    def set_state(g, mx, den, acc):
        m_ref[g] = jnp.broadcast_to(mx, (rows, LANES))
        l_ref[g] = jnp.broadcast_to(den, (rows, LANES))
        acc_ref[g] = acc

    @pl.when(c == 0)
    def _():
        s_new = []
        for g, (_, dil) in enumerate(DSW_PATTERNS):
            qh_ref[g] = _head_rows(q_ref[g])
            _new_rows(ks_ref.at[g], kn_ref.at[g], ns)
            _new_rows(vs_ref.at[g], vn_ref.at[g], ns)
            new_ok = (col <= step) if dil == 1 else (col == step)
            s_new.append(jnp.where(new_ok, _dot_nt(qh_ref[g], ks_ref[g].astype(BF16)), NEG_BIG))
        for g, (kt_ref, vt_ref) in enumerate(((k0_ref, v0_ref), (k1_ref, v1_ref))):
            s_buf = buffer_scores(g, kt_ref)
            mx = jnp.maximum(row_max(s_new[g]), row_max(s_buf))
            p_new, p_buf = jnp.exp(s_new[g] - mx), jnp.exp(s_buf - mx)
            set_state(g, mx, row_sum(p_new) + row_sum(p_buf),
                      _dot(p_new.astype(BF16), vs_ref[g].astype(BF16))
                      + _dot_nt(p_buf.astype(BF16), vt_ref[...].astype(BF16)))
        mx = row_max(s_new[2])
        p_new = jnp.exp(s_new[2] - mx)
        set_state(2, mx, row_sum(p_new), _dot(p_new.astype(BF16), vs_ref[2].astype(BF16)))

    _softmax_accumulate(m_ref.at[2], l_ref.at[2], acc_ref.at[2], buffer_scores(2, k2_ref),
                        lambda p: _dot_nt(p, v2_ref[...].astype(BF16)))

    @pl.when(c == pl.num_programs(1) - 1)
    def _():
        lses = [m_ref[g][:, 0:1] + jnp.log(l_ref[g][:, 0:1]) for g in range(N_GROUPS_D)]
        mx = jnp.maximum(jnp.maximum(lses[0], lses[1]), lses[2])
        es = [jnp.exp(l - mx) for l in lses]
        merged = sum(es[g] * (acc_ref[g] / l_ref[g][:, 0:1]) for g in range(N_GROUPS_D)) / (es[0] + es[1] + es[2])
        out = _own_head(merged, ns)
        for s in range(ns):
            o_ref[s:s + 1, :] = out[s]


def _dsw_sample(q_s, k_new, v_new, buffers_t, *, nb, ns):
    rows = ns * N_HEADS
    tok = pl.BlockSpec((N_GROUPS_D, None, ns, D_MODEL), lambda b, c: (0, b, 0, 0))
    specs = []
    for g, (win, dil) in enumerate(DSW_PATTERNS):
        assert buffers_t[2 * g].shape[2] == win and win // dil == PAGE_SIZE and (dil == 1 or ns <= dil)
        assert dil & (dil - 1) == 0 and DSW_KEY_CHUNK % dil == 0
    w0, w1, w2 = (w for w, _ in DSW_PATTERNS)
    assert max(w0, w1) <= DSW_KEY_CHUNK and w2 % DSW_KEY_CHUNK == 0
    whole = lambda w: pl.BlockSpec((None, D_MODEL, w), lambda b, c: (b, 0, 0))
    chunk = pl.BlockSpec((None, D_MODEL, DSW_KEY_CHUNK), lambda b, c: (b, 0, c))
    specs = [whole(w0), whole(w0), whole(w1), whole(w1), chunk, chunk]
    state = lambda lanes: pltpu.VMEM((N_GROUPS_D, rows, lanes), F32)
    state_rows = lambda n: pltpu.VMEM((N_GROUPS_D, n, D_MODEL), F32)
    return pl.pallas_call(
        functools.partial(_dsw_sample_body, ns=ns),
        grid=(nb, w2 // DSW_KEY_CHUNK),
        in_specs=[tok, tok, tok] + specs,
        out_specs=pl.BlockSpec((None, ns, D_MODEL), lambda b, c: (b, 0, 0)),
        out_shape=jax.ShapeDtypeStruct((nb, ns, D_MODEL), F32),
        scratch_shapes=[pltpu.VMEM((N_GROUPS_D, rows, D_MODEL), BF16), state(LANES), state(LANES), state(D_MODEL),
                        state_rows(PAGE_SIZE), state_rows(PAGE_SIZE)],
        compiler_params=_params("parallel", "arbitrary"),
        name="dsw_sample",
    )(q_s, k_new, v_new, *buffers_t)


def _positions_minor(a):
    return a.transpose(0, 2, 3, 1).reshape(a.shape[0], D_MODEL, a.shape[1])


def _positions_major(at):
    return at.reshape(at.shape[0], N_HEADS, HEAD_DIM, at.shape[2]).transpose(0, 3, 1, 2)


def _steps_minor(at, nb):
    return at.reshape(at.shape[0], N_HEADS, HEAD_DIM, nb).transpose(3, 0, 1, 2)


def kernel(x_prompt, x_sample, cache_sb_k, cache_sb_v, cache_dsw0_k, cache_dsw0_v, cache_dsw1_k, cache_dsw1_v, cache_dsw2_k, cache_dsw2_v, state_conv, page_table, p_prompt, p_sample, g_mix, g_ffn, g_ple, w_ff1, w_ff2, w_ple_in, w_ple_gate, b_ple_gate, w_a_in, g_a_v, w_a_s, b_a_s, w_a_out, w_b_in, w_b_dw, b_b_dw, g_b_ln, b_b_ln, w_b_out, w_c_qkv, b_c_q, b_c_k, w_c_o, w_d_qkv, g_d_q, g_d_k, w_d_o):
    nb_p, seq, _ = x_prompt.shape
    nb_s, ns, _ = x_sample.shape
    depth = g_mix.shape[0]
    mp, ms = nb_p * seq, nb_s * ns
    tm = ROW_TILE
    assert mp % tm == 0 and ms == tm and seq % tm == 0 and nb_s == LANES and mp % FFN_ROW_TILE == 0

    bf = lambda w: w.astype(BF16)
    w1b, w2b, wpib, wpgb = bf(w_ff1), bf(w_ff2), bf(w_ple_in), bf(w_ple_gate)
    xp = x_prompt.reshape(mp, D_MODEL)
    xs = x_sample.reshape(ms, D_MODEL)
    pp = p_prompt.reshape(depth, mp, PLE_DIM)
    ps = p_sample.reshape(depth, ms, PLE_DIM)
    to_sb = lambda a: a.reshape(nb_s, ns, D_MODEL).swapaxes(0, 1).reshape(ms, D_MODEL)
    from_sb = lambda a: a.reshape(ns, nb_s, D_MODEL).swapaxes(0, 1).reshape(ms, D_MODEL)

    wa_in, wa_out = bf(w_a_in), bf(w_a_out)
    xp, vn_p = _mixer_a(xp, g_mix[0], wa_in, g_a_v, w_a_s, b_a_s, wa_out, chunk=CHUNK, tm=tm)
    xs, vn_s = _mixer_a(xs, g_mix[0], wa_in, g_a_v, w_a_s, b_a_s, wa_out, chunk=ns, tm=tm)
    chunk_v_p = vn_p.reshape(nb_p, seq, D_GATE)[:, (seq - 1) // CHUNK * CHUNK:]
    chunk_v_s = vn_s.reshape(nb_s, ns, D_GATE)

    def channel(xp, xs, i):
        args = (g_ffn, w1b, w2b, g_ple, wpib, wpgb, b_ple_gate)
        return (_ffn_ple(xp, pp, i, *args, tm=FFN_ROW_TILE, tf=FFN_COL_TILE),
                _ffn_ple(xs, ps, i, *args, tm=tm, tf=FFN_COL_TILE))

    xp, xs = channel(xp, xs, 0)

    wb_in, wb_out = bf(w_b_in), bf(w_b_out)
    conv_args = (g_mix[1], wb_in, w_b_dw, b_b_dw, g_b_ln, b_b_ln, wb_out)
    xp, conv_p = _mixer_b_prompt(xp, *conv_args, seq=seq, tm=tm)
    state_t = state_conv.swapaxes(0, 1)
    xs_sb, c_sb = _mixer_b_sample(to_sb(xs), state_t, *conv_args, nb=nb_s, ns=ns)
    xs = from_sb(xs_sb)
    conv_s = jnp.concatenate([state_t[ns:], c_sb.reshape(ns, nb_s, D_MODEL)], axis=0).swapaxes(0, 1)
    xp, xs = channel(xp, xs, 1)

    wc_o = bf(w_c_o)
    wc = w_c_qkv.reshape(D_MODEL, 3, D_MODEL)
    wc_kv_t = bf(wc[:, 1:].transpose(1, 2, 0))
    zeros = jnp.zeros((D_MODEL,), F32)
    ones_rows = jnp.ones((1, 3, 1, D_MODEL), F32)
    bias_rows = jnp.stack([b_c_q.reshape(-1), b_c_k.reshape(-1), zeros]).reshape(1, 3, 1, D_MODEL)
    bias_cols = _col(jnp.stack([b_c_k.reshape(-1), zeros]))
    gain_cols = jnp.ones((2, HEAD_DIM, LANES), F32)
    (qp,) = _proj_rows(xp, g_mix[2], bf(wc[:, 0]), bias_rows[:, :1], ones_rows[:, :1], n_groups=1,
                       has_bias=(True,), normed=(False,), scales=(ATTN_SCALE * LOG2E,), dtypes=(BF16,), tm=tm)
    flat = pl.BlockSpec((None, D_MODEL, tm), lambda i: (i // (seq // tm), 0, i % (seq // tm)))
    blocked = pl.BlockSpec((tm // SB_BLOCK, D_MODEL, SB_BLOCK), lambda i: (i, 0, 0))
    sb_kt_p, sb_vt_p, ktb, vtb = _proj_cols(
        xp, g_mix[2], wc_kv_t, bias_cols, gain_cols, has_bias=(True, False), normed=(False, False), tm=tm,
        grid=(mp // tm,), x_map=lambda i: (i, 0), out_specs=[flat, flat, blocked, blocked],
        out_shape=[jax.ShapeDtypeStruct((nb_p, D_MODEL, seq), F32)] * 2
        + [jax.ShapeDtypeStruct((mp // SB_BLOCK, D_MODEL, SB_BLOCK), BF16)] * 2)
    xp = _proj_res(xp, _sb_prompt(qp[0], ktb, vtb, nb=nb_p, seq=seq), wc_o, tm=tm)
    qs, ks_new, vs_new = _proj_rows(xs, g_mix[2], bf(w_c_qkv), bias_rows, ones_rows, n_groups=1,
                                    has_bias=(True, True, False), normed=(False,) * 3,
                                    scales=(ATTN_SCALE, 1.0, 1.0), dtypes=(F32,) * 3, tm=tm)
    step_blocks = pl.BlockSpec((ns, D_MODEL, LANES), lambda i: (0, 0, 0))
    sb_kt_s, sb_vt_s = _proj_cols(
        to_sb(xs), g_mix[2], wc_kv_t, bias_cols, gain_cols, has_bias=(True, False), normed=(False, False), tm=tm,
        grid=(1,), x_map=lambda i: (0, 0), out_specs=[step_blocks] * 2,
        out_shape=[jax.ShapeDtypeStruct((ns, D_MODEL, nb_s), F32)] * 2)
    tok = lambda a: a.reshape(nb_s, ns, D_MODEL)
    o_s = _sb_sample(tok(qs[0]), tok(ks_new[0]), tok(vs_new[0]), _positions_minor(cache_sb_k),
                     _positions_minor(cache_sb_v), page_table, nb=nb_s, ns=ns)
    xs = _proj_res(xs, o_s.reshape(ms, D_MODEL), wc_o, tm=tm)
    xp, xs = channel(xp, xs, 2)

    wd_qkv, wd_o = bf(w_d_qkv), bf(w_d_o)
    wd = w_d_qkv.reshape(D_MODEL, N_GROUPS_D, 3, D_MODEL)
    tile_heads = lambda g_: jnp.tile(g_, (1, N_HEADS))
    gain_rows = jnp.stack([tile_heads(g_d_q), tile_heads(g_d_k), jnp.ones((N_GROUPS_D, D_MODEL), F32)],
                          axis=1).reshape(N_GROUPS_D, 3, 1, D_MODEL)
    zero_rows = jnp.zeros((N_GROUPS_D, 3, 1, D_MODEL), F32)
    rows_args = dict(n_groups=N_GROUPS_D, has_bias=(False,) * 3, normed=(True, True, False),
                     scales=(ATTN_SCALE, 1.0, 1.0), tm=tm)
    qkv_p = _proj_rows_dsw(xp, g_mix[3], wd_qkv, gain_rows[:, :2], nb=nb_p, seq=seq, tm=tm)
    outs, lses = zip(*[_dsw_prompt(*qkv_p[g], dil, nb=nb_p, seq=seq) for g, (_, dil) in enumerate(DSW_PATTERNS)])
    xp_attn = _merge_proj(xp, outs, lses, wd_o, seq=seq, tm=tm)
    qs, ks_new, vs_new = _proj_rows(xs, g_mix[3], wd_qkv, zero_rows, gain_rows, dtypes=(F32,) * 3, **rows_args)
    xs_sb = to_sb(xs)
    rows_p, rows_s = [], []
    for g, (win, _) in enumerate(DSW_PATTERNS):
        wt = bf(wd[:, g, 1:].transpose(1, 2, 0))
        gain = jnp.stack([jnp.broadcast_to(g_d_k[g][:, None], (HEAD_DIM, LANES)), jnp.ones((HEAD_DIM, LANES), F32)])
        no_bias = jnp.zeros((2, D_MODEL, LANES), F32)
        keep = min(win, seq)
        tw = min(tm, keep)
        first = (seq - keep) // tw
        kt, vt = _proj_cols(
            xp, g_mix[3], wt, no_bias, gain, has_bias=(False, False), normed=(True, False), tm=tw,
            grid=(nb_p, keep // tw), x_map=lambda b, j, first=first, per=seq // tw: (b * per + first + j, 0),
            out_specs=[pl.BlockSpec((None, D_MODEL, tw), lambda b, j: (b, 0, j))] * 2,
            out_shape=[jax.ShapeDtypeStruct((nb_p, D_MODEL, keep), F32)] * 2)
        rows_p += [_positions_major(kt), _positions_major(vt)]
        kt, vt = _proj_cols(
            xs_sb, g_mix[3], wt, no_bias, gain, has_bias=(False, False), normed=(True, False), tm=tm,
            grid=(1,), x_map=lambda i: (0, 0), out_specs=[step_blocks] * 2,
            out_shape=[jax.ShapeDtypeStruct((ns, D_MODEL, nb_s), F32)] * 2)
        rows_s += [_steps_minor(kt, nb_s), _steps_minor(vt, nb_s)]
    tok3 = lambda a: a.reshape(N_GROUPS_D, nb_s, ns, D_MODEL)
    buffers_t = [_positions_minor(b) for b in
                 (cache_dsw0_k, cache_dsw0_v, cache_dsw1_k, cache_dsw1_v, cache_dsw2_k, cache_dsw2_v)]
    o_s = _dsw_sample(tok3(qs), tok3(ks_new), tok3(vs_new), buffers_t, nb=nb_s, ns=ns)
    xs = _proj_res(xs, o_s.reshape(ms, D_MODEL), wd_o, tm=tm)
    xp, xs = channel(xp_attn, xs, 3)

    return (xp.reshape(nb_p, seq, D_MODEL), xs.reshape(nb_s, ns, D_MODEL), chunk_v_p, chunk_v_s,
            conv_p, conv_s,
            _positions_major(sb_kt_p), _positions_major(sb_vt_p),
            _steps_minor(sb_kt_s, nb_s), _steps_minor(sb_vt_s, nb_s),
            *rows_p, *rows_s)
```

```python
import functools
import math

import jax
import jax.numpy as jnp
from jax import lax
from jax.experimental import pallas as pl
from jax.experimental.pallas import tpu as pltpu

F32 = jnp.float32
BF16 = jnp.bfloat16

D_MODEL = 1024
D_FF = 4 * D_MODEL
PLE_DIM = 256
EPS = 1e-6
CHUNK = 128
D_GATE = 2 * D_MODEL
N_GROUPS_A = 8
CONV_WIDTH = 31
CONV_PAST = CONV_WIDTH - 1
HEAD_DIM = 64
N_HEADS = D_MODEL // HEAD_DIM
Q_BLOCK = 128
PAGE_SIZE = 128
ATTN_SCALE = HEAD_DIM ** -0.5
DSW_PATTERNS = ((128, 1), (512, 4), (2048, 16))
N_GROUPS_D = len(DSW_PATTERNS)
NEG_BIG = -1e30
LANES = 128
SUBLANES = 8

V7X_VMEM_BYTES = 64 * 1024 * 1024
VMEM_LIMIT = V7X_VMEM_BYTES - 8 * 1024 * 1024
ROW_TILE = 512
FFN_ROW_TILE = 1024
FFN_COL_TILE = 1024
CONV_HALO = 32
CONV_ROWS = 64
SB_BLOCK = 256
SB_GROUP = 4
SB_PAGES_PER_STEP = 8
LOG2E = math.log2(math.e)
DSW_KEY_CHUNK = 1024


def _params(*sem):
    return pltpu.CompilerParams(dimension_semantics=sem, vmem_limit_bytes=VMEM_LIMIT)


def _const_spec(shape):
    nd = len(shape)
    return pl.BlockSpec(shape, lambda *_: (0,) * nd, pipeline_mode=pl.Buffered(1))


def _dot(a, b):
    return jnp.dot(a, b, preferred_element_type=F32)


def _dot_nt(a, b):
    return lax.dot_general(a, b, (((1,), (1,)), ((), ())), preferred_element_type=F32)


def _rms(x, g):
    return x * lax.rsqrt(jnp.mean(x * x, axis=-1, keepdims=True) + EPS) * g


def _sigmoid(x):
    return 1.0 / (1.0 + jnp.exp(-x))


def _gelu(x):
    return 0.5 * x * (1.0 + jnp.tanh(math.sqrt(2.0 / math.pi) * (x + 0.044715 * (x * x * x))))


def _lane_tile(a, reps):
    return a if reps == 1 else jnp.concatenate([a] * reps, axis=1)


def _ffn_ple_body(x_ref, p_ref, gf_ref, w1_ref, w2_ref, gp_ref, wpi_ref, wpg_ref, bpg_ref,
                  o_ref, xn_ref, acc_ref):
    j = pl.program_id(1)

    @pl.when(j == 0)
    def _():
        x = x_ref[...]
        xn_ref[...] = _rms(x, gf_ref[...]).astype(BF16)
        acc_ref[...] = x

    h = jnp.square(jnp.maximum(_dot(xn_ref[...], w1_ref[...]), 0.0))
    acc_ref[...] += _dot(h.astype(BF16), w2_ref[...])

    @pl.when(j == pl.num_programs(1) - 1)
    def _():
        x2 = acc_ref[...]
        gate = _sigmoid(_dot(_rms(x2, gp_ref[...]).astype(BF16), wpg_ref[...]) + bpg_ref[...])
        o_ref[...] = x2 + gate * _dot(p_ref[...].astype(BF16), wpi_ref[...])


def _ffn_ple(x, p, layer, g_ffn, w1b, w2b, g_ple, wpib, wpgb, b_pg, *, tm, tf):
    m = x.shape[0]
    grid = (m // tm, D_FF // tf)
    row = lambda i, j: (i, 0)
    vec = pl.BlockSpec((1, D_MODEL), lambda i, j: (0, 0))
    return pl.pallas_call(
        _ffn_ple_body,
        grid=grid,
        in_specs=[
            pl.BlockSpec((tm, D_MODEL), row),
            pl.BlockSpec((None, tm, PLE_DIM), lambda i, j: (layer, i, 0)),
            vec,
            pl.BlockSpec((None, D_MODEL, tf), lambda i, j: (layer, 0, j)),
            pl.BlockSpec((None, tf, D_MODEL), lambda i, j: (layer, j, 0)),
            vec,
            pl.BlockSpec((None, PLE_DIM, D_MODEL), lambda i, j: (layer, 0, 0)),
            pl.BlockSpec((None, D_MODEL, D_MODEL), lambda i, j: (layer, 0, 0)),
            vec,
        ],
        out_specs=pl.BlockSpec((tm, D_MODEL), row),
        out_shape=jax.ShapeDtypeStruct((m, D_MODEL), F32),
        scratch_shapes=[pltpu.VMEM((tm, D_MODEL), BF16), pltpu.VMEM((tm, D_MODEL), F32)],
        compiler_params=_params("parallel", "arbitrary"),
        name="ffn_ple",
    )(x, p, g_ffn[layer][None], w1b, w2b, g_ple[layer][None], wpib, wpgb, b_pg[layer][None])


def _mixer_a_body(x_ref, g_ref, win_ref, gv_ref, s_ref, bs_ref, wout_ref, o_ref, vn_ref, gated_ref):
    x = x_ref[...]
    h = _rms(x, g_ref[...]).astype(BF16)
    v = _gelu(_dot(h, win_ref[:, D_GATE:]))
    vn = v * lax.rsqrt(jnp.mean(v * v, axis=-1, keepdims=True) + EPS) * gv_ref[...]
    vn_ref[...] = vn
    cg = D_GATE // N_GROUPS_A
    for g in range(N_GROUPS_A):
        cols = slice(g * cg, (g + 1) * cg)
        u = _gelu(_dot(h, win_ref[:, cols]))
        mixed = _dot(s_ref[g], vn_ref[:, cols].astype(BF16)) + _lane_tile(bs_ref[g], cg // LANES)
        gated_ref[:, cols] = (u * mixed).astype(BF16)
    o_ref[...] = x + _dot(gated_ref[...], wout_ref[...])


def _mixer_a(x, g_mix, w_in_b, g_v, w_s, b_s, w_out_b, *, chunk, tm):
    m = x.shape[0]
    reps = tm // chunk
    pos = jnp.arange(tm)
    same = (pos[:, None] // chunk) == (pos[None, :] // chunk)
    causal = pos[None, :] <= pos[:, None]
    w_small = w_s[:, :chunk, :chunk]
    s_mat = jnp.where((same & causal)[None], jnp.tile(w_small, (1, reps, reps)), 0.0).astype(BF16)
    bias = jnp.broadcast_to(jnp.tile(b_s[:, :chunk], (1, reps))[:, :, None], (N_GROUPS_A, tm, LANES))
    row = lambda i: (i, 0)
    return pl.pallas_call(
        _mixer_a_body,
        grid=(m // tm,),
        in_specs=[
            pl.BlockSpec((tm, D_MODEL), row),
            _const_spec((1, D_MODEL)),
            _const_spec((D_MODEL, 2 * D_GATE)),
            _const_spec((1, D_GATE)),
            _const_spec((N_GROUPS_A, tm, tm)),
            _const_spec((N_GROUPS_A, tm, LANES)),
            _const_spec((D_GATE, D_MODEL)),
        ],
        out_specs=[pl.BlockSpec((tm, D_MODEL), row), pl.BlockSpec((tm, D_GATE), row)],
        out_shape=[jax.ShapeDtypeStruct((m, D_MODEL), F32), jax.ShapeDtypeStruct((m, D_GATE), F32)],
        scratch_shapes=[pltpu.VMEM((tm, D_GATE), BF16)],
        compiler_params=_params("parallel"),
        name="mixer_a",
    )(x, g_mix[None], w_in_b, g_v[None], s_mat, bias, w_out_b)


def _glu(h, win_ref):
    return _dot(h, win_ref[:, :D_MODEL]) * _sigmoid(_dot(h, win_ref[:, D_MODEL:]))


def _ln_silu(y, g, b):
    mu = jnp.mean(y, axis=-1, keepdims=True)
    yc = y - mu
    z = yc * lax.rsqrt(jnp.mean(yc * yc, axis=-1, keepdims=True) + EPS) * g + b
    return z * _sigmoid(z)


def _mixer_b_prompt_body(x_ref, g_ref, win_ref, wdw_ref, bdw_ref, gln_ref, bln_ref, wout_ref,
                         o_ref, conv_ref, ext_ref, shift_ref, y_ref, act_ref, *, tm, tiles_per_seq):
    i = pl.program_id(0)
    x = x_ref[...]
    c = _glu(_rms(x, g_ref[...]).astype(BF16), win_ref)

    @pl.when(i % tiles_per_seq == 0)
    def _():
        ext_ref[0:CONV_HALO, :] = jnp.zeros((CONV_HALO, D_MODEL), F32)

    @pl.when(i % tiles_per_seq != 0)
    def _():
        ext_ref[0:CONV_HALO, :] = ext_ref[tm:tm + CONV_HALO, :]

    ext_ref[CONV_HALO:CONV_HALO + tm, :] = c
    conv_ref[...] = c[tm - CONV_PAST:, :]
    span = shift_ref.shape[1]
    for r in range(1, SUBLANES):
        shift_ref[r - 1] = ext_ref[r:r + span, :]
    base = CONV_HALO - CONV_PAST
    for c in range(D_MODEL // LANES):
        lanes = slice(c * LANES, (c + 1) * LANES)
        taps = [wdw_ref[j:j + 1, lanes] for j in range(CONV_WIDTH)]
        for r0 in range(0, tm, CONV_ROWS):
            acc = jnp.broadcast_to(bdw_ref[:, lanes], (CONV_ROWS, LANES))
            for j in range(CONV_WIDTH):
                r = (base + j) % SUBLANES
                lo = base + j - r + r0
                rows = ext_ref[lo:lo + CONV_ROWS, lanes] if r == 0 else shift_ref[r - 1, lo:lo + CONV_ROWS, lanes]
                acc = acc + taps[j] * rows
            y_ref[r0:r0 + CONV_ROWS, lanes] = acc
    for r0 in range(0, tm, CONV_ROWS):
        act_ref[r0:r0 + CONV_ROWS, :] = _ln_silu(y_ref[r0:r0 + CONV_ROWS, :], gln_ref[...], bln_ref[...]).astype(BF16)
    o_ref[...] = x + _dot(act_ref[...], wout_ref[...])


def _mixer_b_prompt(x, g_mix, w_in_b, w_dw, b_dw, g_ln, b_ln, w_out_b, *, seq, tm):
    m = x.shape[0]
    nb = m // seq
    tiles_per_seq = seq // tm
    row = lambda i: (i, 0)
    vec = _const_spec((1, D_MODEL))
    return pl.pallas_call(
        functools.partial(_mixer_b_prompt_body, tm=tm, tiles_per_seq=tiles_per_seq),
        grid=(m // tm,),
        in_specs=[
            pl.BlockSpec((tm, D_MODEL), row),
            vec,
            _const_spec((D_MODEL, 2 * D_MODEL)),
            _const_spec((CONV_WIDTH, D_MODEL)),
            vec, vec, vec,
            _const_spec((D_MODEL, D_MODEL)),
        ],
        out_specs=[pl.BlockSpec((tm, D_MODEL), row),
                   pl.BlockSpec((None, CONV_PAST, D_MODEL), lambda i: (i // tiles_per_seq, 0, 0))],
        out_shape=[jax.ShapeDtypeStruct((m, D_MODEL), F32),
                   jax.ShapeDtypeStruct((nb, CONV_PAST, D_MODEL), F32)],
        scratch_shapes=[pltpu.VMEM((tm + CONV_HALO, D_MODEL), F32),
                        pltpu.VMEM((SUBLANES - 1, tm + CONV_HALO - SUBLANES, D_MODEL), F32),
                        pltpu.VMEM((tm, D_MODEL), F32), pltpu.VMEM((tm, D_MODEL), BF16)],
        compiler_params=_params("arbitrary"),
        name="mixer_b_prompt",
    )(x, g_mix[None], w_in_b, w_dw, b_dw[None], g_ln[None], b_ln[None], w_out_b)


def _mixer_b_sample_body(x_ref, st_ref, g_ref, win_ref, wdw_ref, bdw_ref, gln_ref, bln_ref, wout_ref,
                         o_ref, c_ref, act_ref, *, nb, ns):
    x = x_ref[...]
    c_ref[...] = _glu(_rms(x, g_ref[...]).astype(BF16), win_ref)
    for s in range(ns):
        acc = jnp.broadcast_to(bdw_ref[...], (nb, D_MODEL))
        for j in range(CONV_WIDTH):
            k = s + j
            src = st_ref[k] if k < CONV_PAST else c_ref[(k - CONV_PAST) * nb:(k - CONV_PAST + 1) * nb, :]
            acc = acc + wdw_ref[j:j + 1, :] * src
        act_ref[s * nb:(s + 1) * nb, :] = _ln_silu(acc, gln_ref[...], bln_ref[...]).astype(BF16)
    o_ref[...] = x + _dot(act_ref[...], wout_ref[...])


def _mixer_b_sample(x_sb, state_t, g_mix, w_in_b, w_dw, b_dw, g_ln, b_ln, w_out_b, *, nb, ns):
    m = nb * ns
    vec = _const_spec((1, D_MODEL))
    full = _const_spec((m, D_MODEL))
    return pl.pallas_call(
        functools.partial(_mixer_b_sample_body, nb=nb, ns=ns),
        grid=(1,),
        in_specs=[full, _const_spec((CONV_PAST, nb, D_MODEL)), vec,
                  _const_spec((D_MODEL, 2 * D_MODEL)), _const_spec((CONV_WIDTH, D_MODEL)),
                  vec, vec, vec, _const_spec((D_MODEL, D_MODEL))],
        out_specs=[pl.BlockSpec((m, D_MODEL), lambda i: (0, 0)), pl.BlockSpec((m, D_MODEL), lambda i: (0, 0))],
        out_shape=[jax.ShapeDtypeStruct((m, D_MODEL), F32), jax.ShapeDtypeStruct((m, D_MODEL), F32)],
        scratch_shapes=[pltpu.VMEM((m, D_MODEL), BF16)],
        compiler_params=_params("arbitrary"),
        name="mixer_b_sample",
    )(x_sb, state_t, g_mix[None], w_in_b, w_dw, b_dw[None], g_ln[None], b_ln[None], w_out_b)


def _proj_res_body(x_ref, a_ref, w_ref, o_ref):
    o_ref[...] = x_ref[...] + _dot(a_ref[...].astype(BF16), w_ref[...])


def _proj_res(x, a, w_b, *, tm):
    m = x.shape[0]
    row = lambda i: (i, 0)
    return pl.pallas_call(
        _proj_res_body,
        grid=(m // tm,),
        in_specs=[pl.BlockSpec((tm, D_MODEL), row), pl.BlockSpec((tm, D_MODEL), row),
                  _const_spec((D_MODEL, D_MODEL))],
        out_specs=pl.BlockSpec((tm, D_MODEL), row),
        out_shape=jax.ShapeDtypeStruct((m, D_MODEL), F32),
        compiler_params=_params("parallel"),
        name="proj_res",
    )(x, a, w_b)


def _proj_rows_body(x_ref, g_ref, w_ref, bias_ref, gain_ref, hs_ref, *rest, has_bias, normed, scales):
    outs, h_ref = rest[:-1], rest[-1]

    @pl.when(pl.program_id(1) == 0)
    def _():
        h_ref[...] = _rms(x_ref[...], g_ref[...]).astype(BF16)

    h = h_ref[...]
    for j, out in enumerate(outs):
        y = _dot(h, w_ref[:, j * D_MODEL:(j + 1) * D_MODEL])
        if has_bias[j]:
            y = y + bias_ref[j]
        if normed[j]:
            y = y * lax.rsqrt(_dot((y * y).astype(BF16), hs_ref[...]) + EPS) * gain_ref[j]
        out[...] = (y * scales[j]).astype(out.dtype)


def _proj_rows(x, g_mix, w_b, bias, gain, *, n_groups, has_bias, normed, scales, dtypes, tm):
    m = x.shape[0]
    n_out = len(dtypes)
    lane_head = jnp.arange(D_MODEL) // HEAD_DIM
    head_mean = jnp.where(lane_head[:, None] == lane_head[None, :], 1.0 / HEAD_DIM, 0.0).astype(BF16)
    per_group = pl.BlockSpec((None, n_out, 1, D_MODEL), lambda i, g: (g, 0, 0, 0))
    out = pl.BlockSpec((None, tm, D_MODEL), lambda i, g: (g, i, 0))
    return pl.pallas_call(
        functools.partial(_proj_rows_body, has_bias=has_bias, normed=normed, scales=scales),
        grid=(m // tm, n_groups),
        in_specs=[pl.BlockSpec((tm, D_MODEL), lambda i, g: (i, 0)),
                  pl.BlockSpec((1, D_MODEL), lambda i, g: (0, 0)),
                  pl.BlockSpec((D_MODEL, n_out * D_MODEL), lambda i, g: (0, g)),
                  per_group, per_group,
                  pl.BlockSpec((D_MODEL, D_MODEL), lambda i, g: (0, 0))],
        out_specs=[out] * n_out,
        out_shape=[jax.ShapeDtypeStruct((n_groups, m, D_MODEL), dt) for dt in dtypes],
        scratch_shapes=[pltpu.VMEM((tm, D_MODEL), BF16)],
        compiler_params=_params("parallel", "arbitrary"),
        name="proj_rows",
    )(x, g_mix[None], w_b, bias, gain, head_mean)


def _store_cols(ref, yt):
    if len(ref.shape) == 2:
        ref[...] = yt.astype(ref.dtype)
    else:
        width = ref.shape[2]
        for c in range(ref.shape[0]):
            ref[c] = yt[:, c * width:(c + 1) * width].astype(ref.dtype)


def _proj_cols_body(x_ref, g_ref, wt_ref, bias_ref, gain_ref, *outs, n_out, has_bias, normed):
    h = _rms(x_ref[...], g_ref[...]).astype(BF16)
    reps = h.shape[0] // LANES
    for j in range(n_out):
        yt = _dot_nt(wt_ref[j], h)
        if has_bias[j]:
            yt = yt + _lane_tile(bias_ref[j], reps)
        if normed[j]:
            gain = _lane_tile(gain_ref[j], reps)
            heads = []
            for hh in range(N_HEADS):
                blk = yt[hh * HEAD_DIM:(hh + 1) * HEAD_DIM, :]
                heads.append(blk * lax.rsqrt(jnp.mean(blk * blk, axis=0, keepdims=True) + EPS) * gain)
            yt = jnp.concatenate(heads, axis=0)
        for out in outs[j::n_out]:
            _store_cols(out, yt)


def _proj_cols(x, g_mix, wt_b, bias_col, gain_col, *, has_bias, normed, tm, grid, x_map, out_specs, out_shape):
    n_out = wt_b.shape[0]
    return pl.pallas_call(
        functools.partial(_proj_cols_body, n_out=n_out, has_bias=has_bias, normed=normed),
        grid=grid,
        in_specs=[pl.BlockSpec((tm, D_MODEL), x_map),
                  _const_spec((1, D_MODEL)), _const_spec(wt_b.shape),
                  _const_spec(bias_col.shape), _const_spec(gain_col.shape)],
        out_specs=out_specs,
        out_shape=out_shape,
        compiler_params=_params(*(["parallel"] * len(grid))),
        name="proj_cols",
    )(x, g_mix[None], wt_b, bias_col, gain_col)


def _col(v):
    return jnp.broadcast_to(v[:, :, None], v.shape + (LANES,))


def _sb_block(z, tri, valid):
    n = z.shape[1]
    log_keep = -(jnp.maximum(z, 0.0) + jnp.log(1.0 + jnp.exp(-jnp.abs(z))))
    if valid is not None:
        log_keep = jnp.where(valid, log_keep, 0.0)
    hi = log_keep.astype(BF16)
    lo = (log_keep - hi.astype(F32)).astype(BF16)
    sums = _dot(hi, tri) + _dot(lo, tri)
    return z + log_keep + sums[:, :n], sums[:, n:]


def _sb_tri(n):
    j = jnp.arange(n)
    return jnp.concatenate([(j[:, None] > j[None, :]), jnp.ones((n, n), bool)], axis=1).astype(BF16)


def _sb_prompt_body(q_ref, kt_ref, vt_ref, tri_ref, o_ref, acc_ref, carry_ref):
    qb = pl.program_id(2)
    heads = q_ref.shape[1] // HEAD_DIM
    tri = tri_ref[...]
    acc_ref[...] = jnp.zeros(acc_ref.shape, F32)
    carry_ref[...] = jnp.zeros(carry_ref.shape, F32)
    row = lax.broadcasted_iota(jnp.int32, (SB_BLOCK, SB_BLOCK), 0)
    col = lax.broadcasted_iota(jnp.int32, (SB_BLOCK, SB_BLOCK), 1)
    strictly_before = col < row

    def sweep(kbs, diagonal_first):
        outs, carries = [], []
        for hh in range(heads):
            sl = slice(hh * HEAD_DIM, (hh + 1) * HEAD_DIM)
            q = q_ref[:, sl]
            parts = []
            for n, kb in enumerate(kbs):
                valid = strictly_before if (diagonal_first and n == 0) else None
                z = _dot(q, kt_ref[kb, sl, :])
                sp = jnp.maximum(z, 0.0) + jnp.log2(1.0 + jnp.exp2(-jnp.abs(z)))
                if valid is not None:
                    sp = jnp.where(valid, sp, 0.0)
                after = _dot(sp.astype(BF16), tri)
                parts.append((kb, valid, z - sp - after, jnp.sum(sp, axis=-1, keepdims=True)))
            carry = carry_ref[hh]
            o = acc_ref[:, sl]
            for kb, valid, t, total in parts:
                a = jnp.exp2(t - _lane_tile(carry, SB_BLOCK // LANES))
                if valid is not None:
                    a = jnp.where(valid, a, 0.0)
                o = o + _dot_nt(a.astype(BF16), vt_ref[kb, sl, :])
                carry = carry + total
            outs.append(o)
            carries.append(carry)
        acc_ref[...] = jnp.concatenate(outs, axis=1)
        carry_ref[...] = jnp.stack(carries)

    rest = qb % SB_GROUP
    for r in range(SB_GROUP):
        @pl.when(rest == r)
        def _(r=r):
            sweep([qb - n for n in range(r + 1)], True)

    first = qb - rest

    def group(i, _):
        kb = first - 1 - SB_GROUP * i
        sweep([kb - n for n in range(SB_GROUP)], False)
        return 0

    lax.fori_loop(0, qb // SB_GROUP, group, 0)
    o_ref[...] = acc_ref[...]


def _sb_prompt(qs_b, kt_b, vt_b, *, nb, seq):
    lanes = 2 * HEAD_DIM
    nq = seq // SB_BLOCK
    kv = pl.BlockSpec((nq, lanes, SB_BLOCK), lambda b, hp, qb: (b, hp, 0))
    qo = lambda b, hp, qb: (b * nq + qb, hp)
    j = jnp.arange(SB_BLOCK)
    tri = (j[:, None] > j[None, :]).astype(BF16)
    return pl.pallas_call(
        _sb_prompt_body,
        grid=(nb, D_MODEL // lanes, nq),
        in_specs=[pl.BlockSpec((SB_BLOCK, lanes), qo), kv, kv, _const_spec((SB_BLOCK, SB_BLOCK))],
        out_specs=pl.BlockSpec((SB_BLOCK, lanes), qo),
        out_shape=jax.ShapeDtypeStruct((nb * seq, D_MODEL), F32),
        scratch_shapes=[pltpu.VMEM((SB_BLOCK, lanes), F32), pltpu.VMEM((2, SB_BLOCK, LANES), F32)],
        compiler_params=_params("parallel", "parallel", "arbitrary"),
        name="sb_prompt",
    )(qs_b, kt_b, vt_b, tri)


def _head_rows(q):
    ns = q.shape[0]
    rows = jnp.concatenate([jnp.broadcast_to(q[s:s + 1, :], (N_HEADS, D_MODEL)) for s in range(ns)], axis=0)
    r = lax.broadcasted_iota(jnp.int32, rows.shape, 0)
    lane = lax.broadcasted_iota(jnp.int32, rows.shape, 1)
    return jnp.where((lane >> 6) == (r & (N_HEADS - 1)), rows, 0.0).astype(BF16)


def _own_head(full, ns):
    r = lax.broadcasted_iota(jnp.int32, (N_HEADS, D_MODEL), 0)
    lane = lax.broadcasted_iota(jnp.int32, (N_HEADS, D_MODEL), 1)
    own = (lane >> 6) == r
    return [jnp.sum(jnp.where(own, full[s * N_HEADS:(s + 1) * N_HEADS, :], 0.0), axis=0, keepdims=True)
            for s in range(ns)]


def _new_rows(slab_ref, new_ref, ns):
    slab_ref[...] = jnp.zeros(slab_ref.shape, F32)
    slab_ref[0:ns, :] = new_ref[...]


def _sb_sample_body(pt_ref, q_ref, kn_ref, vn_ref, tri_ref, *rest, ns, pages):
    kt_refs, vt_refs = rest[:pages], rest[pages:2 * pages]
    o_ref, qh_ref, acc_ref, carry_ref, ks_ref, vs_ref = rest[2 * pages:]
    c = pl.program_id(1)
    rows = ns * N_HEADS
    tri = tri_ref[...]

    @pl.when(c == 0)
    def _():
        qh_ref[...] = _head_rows(q_ref[...])
        _new_rows(ks_ref, kn_ref, ns)
        _new_rows(vs_ref, vn_ref, ns)
        r = lax.broadcasted_iota(jnp.int32, (rows, PAGE_SIZE), 0)
        col = lax.broadcasted_iota(jnp.int32, (rows, PAGE_SIZE), 1)
        valid = col < (r >> 4)
        t, total = _sb_block(_dot_nt(qh_ref[...], ks_ref[...].astype(BF16)), tri, valid)
        a = jnp.where(valid, jnp.exp(t), 0.0)
        acc_ref[...] = _dot(a.astype(BF16), vs_ref[...].astype(BF16))
        carry_ref[...] = total

    qh = qh_ref[...]
    fronts = [_sb_block(_dot(qh, kt_refs[p][...].astype(BF16)), tri, None) for p in range(pages)]
    carry = carry_ref[...]
    acc = acc_ref[...]
    for p, (t, total) in enumerate(fronts):
        acc = acc + _dot_nt(jnp.exp(t + carry).astype(BF16), vt_refs[p][...].astype(BF16))
        carry = carry + total
    acc_ref[...] = acc
    carry_ref[...] = carry

    @pl.when(c == pl.num_programs(1) - 1)
    def _():
        out = _own_head(acc_ref[...], ns)
        for s in range(ns):
            o_ref[s:s + 1, :] = out[s]


def _sb_sample(q_s, k_new, v_new, cache_kt, cache_vt, page_table, *, nb, ns):
    n_pages = page_table.shape[1]
    pages = SB_PAGES_PER_STEP
    steps = n_pages // pages
    rows = ns * N_HEADS
    tok = pl.BlockSpec((None, ns, D_MODEL), lambda b, c, pt: (b, 0, 0))

    def page_spec(p):
        return pl.BlockSpec((None, D_MODEL, PAGE_SIZE),
                            lambda b, c, pt: (pt[b * n_pages + n_pages - 1 - (c * pages + p)], 0, 0))

    grid_spec = pltpu.PrefetchScalarGridSpec(
        num_scalar_prefetch=1,
        grid=(nb, steps),
        in_specs=[tok, tok, tok, pl.BlockSpec((PAGE_SIZE, 2 * PAGE_SIZE), lambda b, c, pt: (0, 0))]
        + [page_spec(p) for p in range(pages)] * 2,
        out_specs=tok,
        scratch_shapes=[pltpu.VMEM((rows, D_MODEL), BF16), pltpu.VMEM((rows, D_MODEL), F32),
                        pltpu.VMEM((rows, PAGE_SIZE), F32),
                        pltpu.VMEM((PAGE_SIZE, D_MODEL), F32), pltpu.VMEM((PAGE_SIZE, D_MODEL), F32)],
    )
    return pl.pallas_call(
        functools.partial(_sb_sample_body, ns=ns, pages=pages),
        grid_spec=grid_spec,
        out_shape=jax.ShapeDtypeStruct((nb, ns, D_MODEL), F32),
        compiler_params=_params("parallel", "arbitrary"),
        name="sb_sample",
    )(page_table.reshape(-1), q_s, k_new, v_new, _sb_tri(PAGE_SIZE), *([cache_kt] * pages), *([cache_vt] * pages))


def _proj_rows_dsw_body(x_ref, g_ref, w_ref, gain_ref, hs_ref, *rest, dils):
    outs, y_ref = rest[:-1], rest[-1]
    h = _rms(x_ref[...], g_ref[...]).astype(BF16)
    tm = h.shape[0]
    for g, dil in enumerate(dils):
        for j in range(3):
            col = (3 * g + j) * D_MODEL
            y = _dot(h, w_ref[:, col:col + D_MODEL])
            if j < 2:
                y = y * lax.rsqrt(_dot((y * y).astype(BF16), hs_ref[...]) + EPS) * gain_ref[g, j]
            if j == 0:
                y = y * ATTN_SCALE
            out = outs[3 * g + j]
            if dil == 1:
                out[0] = y.astype(BF16)
            else:
                for c in range(D_MODEL // LANES):
                    y_ref[c] = y[:, c * LANES:(c + 1) * LANES]
                for r in range(dil):
                    rows = [y_ref[c, pl.ds(r, tm // dil, stride=dil), :] for c in range(D_MODEL // LANES)]
                    out[r] = jnp.concatenate(rows, axis=1).astype(BF16)


def _proj_rows_dsw(x, g_mix, w_b, gain, *, nb, seq, tm):
    m = x.shape[0]
    dils = tuple(d for _, d in DSW_PATTERNS)
    per_seq = seq // tm
    lane_head = jnp.arange(D_MODEL) // HEAD_DIM
    head_mean = jnp.where(lane_head[:, None] == lane_head[None, :], 1.0 / HEAD_DIM, 0.0).astype(BF16)
    out_specs, out_shape = [], []
    for dil in dils:
        for _ in range(3):
            out_specs.append(pl.BlockSpec((None, dil, tm // dil, D_MODEL),
                                          lambda i: (i // per_seq, 0, i % per_seq, 0)))
            out_shape.append(jax.ShapeDtypeStruct((nb, dil, seq // dil, D_MODEL), BF16))
    outs = pl.pallas_call(
        functools.partial(_proj_rows_dsw_body, dils=dils),
        grid=(m // tm,),
        in_specs=[pl.BlockSpec((tm, D_MODEL), lambda i: (i, 0)), _const_spec((1, D_MODEL)),
                  _const_spec(w_b.shape), _const_spec(gain.shape), _const_spec((D_MODEL, D_MODEL))],
        out_specs=out_specs,
        out_shape=out_shape,
        scratch_shapes=[pltpu.VMEM((D_MODEL // LANES, tm, LANES), F32)],
        compiler_params=_params("parallel"),
        name="proj_rows_dsw",
    )(x, g_mix[None], w_b, gain, head_mean)
    return [outs[3 * g:3 * g + 3] for g in range(len(dils))]


def _dsw_prompt_body(q_ref, kp_ref, kc_ref, vp_ref, vc_ref, o_ref, l_ref):
    n = pl.program_id(2)
    row = lax.broadcasted_iota(jnp.int32, (Q_BLOCK, Q_BLOCK), 0)
    col = lax.broadcasted_iota(jnp.int32, (Q_BLOCK, Q_BLOCK), 1)
    bias = jnp.concatenate([jnp.where((col >= row) & (n > 0), 0.0, NEG_BIG),
                            jnp.where(col <= row, 0.0, NEG_BIG)], axis=1)
    lanes = 2 * HEAD_DIM
    lane = lax.broadcasted_iota(jnp.int32, (1, lanes), 1)
    first = lane < HEAD_DIM
    keep = [first.astype(BF16), (~first).astype(BF16)]
    outs, lses = [], []
    for hp in range(D_MODEL // lanes):
        cols = slice(hp * lanes, (hp + 1) * lanes)
        q2 = q_ref[:, cols]
        k2 = jnp.concatenate([kp_ref[:, cols], kc_ref[:, cols]], axis=0)
        v2 = jnp.concatenate([vp_ref[:, cols], vc_ref[:, cols]], axis=0)
        o2, l2 = [], []
        for hh in range(2):
            s = _dot_nt(q2 * keep[hh], k2) + bias
            mx = jnp.max(s, axis=-1, keepdims=True)
            p = jnp.exp(s - mx)
            den = jnp.sum(p, axis=-1, keepdims=True)
            o2.append(_dot(p.astype(BF16), v2) / den)
            l2.append(mx + jnp.log(den))
        outs.append(jnp.where(first, o2[0], o2[1]))
        lses.append(jnp.where(first, l2[0], l2[1]))
    o_ref[...] = jnp.concatenate(outs, axis=1)
    l_ref[...] = jnp.concatenate(lses, axis=1)


def _dsw_prompt(qs_b, k_b, v_b, dil, *, nb, seq):
    length = seq // dil
    cur = pl.BlockSpec((None, None, Q_BLOCK, D_MODEL), lambda b, r, n: (b, r, n, 0))
    prev = pl.BlockSpec((None, None, Q_BLOCK, D_MODEL), lambda b, r, n: (b, r, jnp.maximum(n - 1, 0), 0))
    sds = jax.ShapeDtypeStruct((nb, dil, length, D_MODEL), F32)
    return pl.pallas_call(
        _dsw_prompt_body,
        grid=(nb, dil, length // Q_BLOCK),
        in_specs=[cur, prev, cur, prev, cur],
        out_specs=[cur, cur],
        out_shape=[sds, sds],
        compiler_params=_params("parallel", "parallel", "arbitrary"),
        name="dsw_prompt",
    )(qs_b, k_b, k_b, v_b, v_b)


def _merge_proj_body(x_ref, *rest, dils):
    n = len(dils)
    o_refs, l_refs, (w_ref, out_ref, tok_ref) = rest[:n], rest[n:2 * n], rest[2 * n:]
    tm = x_ref.shape[0]

    def token_major(ref, dil, slot):
        if dil == 1:
            return ref[0]
        for r in range(dil):
            blk = ref[r]
            for c in range(D_MODEL // LANES):
                tok_ref[slot, c, pl.ds(r, tm // dil, stride=dil), :] = blk[:, c * LANES:(c + 1) * LANES]
        return jnp.concatenate([tok_ref[slot, c] for c in range(D_MODEL // LANES)], axis=1)

    outs = [token_major(o_refs[g], dil, 2 * g) for g, dil in enumerate(dils)]
    lses = [token_major(l_refs[g], dil, 2 * g + 1) for g, dil in enumerate(dils)]
    mx = functools.reduce(jnp.maximum, lses)
    es = [jnp.exp(l - mx) for l in lses]
    o = sum(e * o for e, o in zip(es, outs)) / sum(es)
    out_ref[...] = x_ref[...] + _dot(o.astype(BF16), w_ref[...])


def _merge_proj(x, outs, lses, w_b, *, seq, tm):
    m = x.shape[0]
    dils = tuple(d for _, d in DSW_PATTERNS)
    per_seq = seq // tm
    row = pl.BlockSpec((tm, D_MODEL), lambda i: (i, 0))
    split = [pl.BlockSpec((None, dil, tm // dil, D_MODEL), lambda i: (i // per_seq, 0, i % per_seq, 0))
             for dil in dils]
    return pl.pallas_call(
        functools.partial(_merge_proj_body, dils=dils),
        grid=(m // tm,),
        in_specs=[row] + split + split + [_const_spec((D_MODEL, D_MODEL))],
        out_specs=row,
        out_shape=jax.ShapeDtypeStruct((m, D_MODEL), F32),
        scratch_shapes=[pltpu.VMEM((2 * len(dils), D_MODEL // LANES, tm, LANES), F32)],
        compiler_params=_params("parallel"),
        name="merge_proj",
    )(x, *outs, *lses, w_b)


def _softmax_accumulate(m_ref, l_ref, acc_ref, s, value_dot):
    m_old = m_ref[...]
    m_new = jnp.maximum(m_old, jnp.max(s, axis=-1, keepdims=True))
    alpha = jnp.exp(m_old - m_new)
    p = jnp.exp(s - m_new[:, 0:1])
    l_ref[...] = alpha * l_ref[...] + jnp.sum(p, axis=-1, keepdims=True)
    acc_ref[...] = alpha[:, 0:1] * acc_ref[...] + value_dot(p.astype(BF16))
    m_ref[...] = m_new


def _dsw_sample_body(q_ref, kn_ref, vn_ref, k0_ref, v0_ref, k1_ref, v1_ref, k2_ref, v2_ref, o_ref,
                     qh_ref, m_ref, l_ref, acc_ref, ks_ref, vs_ref, *, ns):
    c = pl.program_id(1)
    rows = ns * N_HEADS
    step = lax.broadcasted_iota(jnp.int32, (rows, PAGE_SIZE), 0) >> 4
    col = lax.broadcasted_iota(jnp.int32, (rows, PAGE_SIZE), 1)

    def buffer_scores(g, kt_ref):
        n, dil = kt_ref.shape[1], DSW_PATTERNS[g][1]
        pos = lax.broadcasted_iota(jnp.int32, (rows, n), 1)
        qstep = lax.broadcasted_iota(jnp.int32, (rows, n), 0) >> 4
        valid = (pos >= qstep) if dil == 1 else ((pos & (dil - 1)) == qstep)
        return jnp.where(valid, _dot(qh_ref[g], kt_ref[...].astype(BF16)), NEG_BIG)

    def row_max(s):
        return jnp.max(s, axis=-1, keepdims=True)

    def row_sum(p):
        return jnp.sum(p, axis=-1, keepdims=True)

    def set_state(g, mx, den, acc):
        m_ref[g] = jnp.broadcast_to(mx, (rows, LANES))
        l_ref[g] = jnp.broadcast_to(den, (rows, LANES))
        acc_ref[g] = acc

    @pl.when(c == 0)
    def _():
        s_new = []
        for g, (_, dil) in enumerate(DSW_PATTERNS):
            qh_ref[g] = _head_rows(q_ref[g])
            _new_rows(ks_ref.at[g], kn_ref.at[g], ns)
            _new_rows(vs_ref.at[g], vn_ref.at[g], ns)
            new_ok = (col <= step) if dil == 1 else (col == step)
            s_new.append(jnp.where(new_ok, _dot_nt(qh_ref[g], ks_ref[g].astype(BF16)), NEG_BIG))
        for g, (kt_ref, vt_ref) in enumerate(((k0_ref, v0_ref), (k1_ref, v1_ref))):
            s_buf = buffer_scores(g, kt_ref)
            mx = jnp.maximum(row_max(s_new[g]), row_max(s_buf))
            p_new, p_buf = jnp.exp(s_new[g] - mx), jnp.exp(s_buf - mx)
            set_state(g, mx, row_sum(p_new) + row_sum(p_buf),
                      _dot(p_new.astype(BF16), vs_ref[g].astype(BF16))
                      + _dot_nt(p_buf.astype(BF16), vt_ref[...].astype(BF16)))
        mx = row_max(s_new[2])
        p_new = jnp.exp(s_new[2] - mx)
        set_state(2, mx, row_sum(p_new), _dot(p_new.astype(BF16), vs_ref[2].astype(BF16)))

    _softmax_accumulate(m_ref.at[2], l_ref.at[2], acc_ref.at[2], buffer_scores(2, k2_ref),
                        lambda p: _dot_nt(p, v2_ref[...].astype(BF16)))

    @pl.when(c == pl.num_programs(1) - 1)
    def _():
        lses = [m_ref[g][:, 0:1] + jnp.log(l_ref[g][:, 0:1]) for g in range(N_GROUPS_D)]
        mx = jnp.maximum(jnp.maximum(lses[0], lses[1]), lses[2])
        es = [jnp.exp(l - mx) for l in lses]
        merged = sum(es[g] * (acc_ref[g] / l_ref[g][:, 0:1]) for g in range(N_GROUPS_D)) / (es[0] + es[1] + es[2])
        out = _own_head(merged, ns)
        for s in range(ns):
            o_ref[s:s + 1, :] = out[s]


def _dsw_sample(q_s, k_new, v_new, buffers_t, *, nb, ns):
    rows = ns * N_HEADS
    tok = pl.BlockSpec((N_GROUPS_D, None, ns, D_MODEL), lambda b, c: (0, b, 0, 0))
    specs = []
    for g, (win, dil) in enumerate(DSW_PATTERNS):
        assert buffers_t[2 * g].shape[2] == win and win // dil == PAGE_SIZE and (dil == 1 or ns <= dil)
        assert dil & (dil - 1) == 0 and DSW_KEY_CHUNK % dil == 0
    w0, w1, w2 = (w for w, _ in DSW_PATTERNS)
    assert max(w0, w1) <= DSW_KEY_CHUNK and w2 % DSW_KEY_CHUNK == 0
    whole = lambda w: pl.BlockSpec((None, D_MODEL, w), lambda b, c: (b, 0, 0))
    chunk = pl.BlockSpec((None, D_MODEL, DSW_KEY_CHUNK), lambda b, c: (b, 0, c))
    specs = [whole(w0), whole(w0), whole(w1), whole(w1), chunk, chunk]
    state = lambda lanes: pltpu.VMEM((N_GROUPS_D, rows, lanes), F32)
    state_rows = lambda n: pltpu.VMEM((N_GROUPS_D, n, D_MODEL), F32)
    return pl.pallas_call(
        functools.partial(_dsw_sample_body, ns=ns),
        grid=(nb, w2 // DSW_KEY_CHUNK),
        in_specs=[tok, tok, tok] + specs,
        out_specs=pl.BlockSpec((None, ns, D_MODEL), lambda b, c: (b, 0, 0)),
        out_shape=jax.ShapeDtypeStruct((nb, ns, D_MODEL), F32),
        scratch_shapes=[pltpu.VMEM((N_GROUPS_D, rows, D_MODEL), BF16), state(LANES), state(LANES), state(D_MODEL),
                        state_rows(PAGE_SIZE), state_rows(PAGE_SIZE)],
        compiler_params=_params("parallel", "arbitrary"),
        name="dsw_sample",
    )(q_s, k_new, v_new, *buffers_t)


def _positions_minor(a):
    return a.transpose(0, 2, 3, 1).reshape(a.shape[0], D_MODEL, a.shape[1])


def _positions_major(at):
    return at.reshape(at.shape[0], N_HEADS, HEAD_DIM, at.shape[2]).transpose(0, 3, 1, 2)


def _steps_minor(at, nb):
    return at.reshape(at.shape[0], N_HEADS, HEAD_DIM, nb).transpose(3, 0, 1, 2)


def kernel(x_prompt, x_sample, cache_sb_k, cache_sb_v, cache_dsw0_k, cache_dsw0_v, cache_dsw1_k, cache_dsw1_v, cache_dsw2_k, cache_dsw2_v, state_conv, page_table, p_prompt, p_sample, g_mix, g_ffn, g_ple, w_ff1, w_ff2, w_ple_in, w_ple_gate, b_ple_gate, w_a_in, g_a_v, w_a_s, b_a_s, w_a_out, w_b_in, w_b_dw, b_b_dw, g_b_ln, b_b_ln, w_b_out, w_c_qkv, b_c_q, b_c_k, w_c_o, w_d_qkv, g_d_q, g_d_k, w_d_o):
    nb_p, seq, _ = x_prompt.shape
    nb_s, ns, _ = x_sample.shape
    depth = g_mix.shape[0]
    mp, ms = nb_p * seq, nb_s * ns
    tm = ROW_TILE
    assert mp % tm == 0 and ms == tm and seq % tm == 0 and nb_s == LANES and mp % FFN_ROW_TILE == 0

    bf = lambda w: w.astype(BF16)
    w1b, w2b, wpib, wpgb = bf(w_ff1), bf(w_ff2), bf(w_ple_in), bf(w_ple_gate)
    xp = x_prompt.reshape(mp, D_MODEL)
    xs = x_sample.reshape(ms, D_MODEL)
    pp = p_prompt.reshape(depth, mp, PLE_DIM)
    ps = p_sample.reshape(depth, ms, PLE_DIM)
    to_sb = lambda a: a.reshape(nb_s, ns, D_MODEL).swapaxes(0, 1).reshape(ms, D_MODEL)
    from_sb = lambda a: a.reshape(ns, nb_s, D_MODEL).swapaxes(0, 1).reshape(ms, D_MODEL)

    wa_in, wa_out = bf(w_a_in), bf(w_a_out)
    xp, vn_p = _mixer_a(xp, g_mix[0], wa_in, g_a_v, w_a_s, b_a_s, wa_out, chunk=CHUNK, tm=tm)
    xs, vn_s = _mixer_a(xs, g_mix[0], wa_in, g_a_v, w_a_s, b_a_s, wa_out, chunk=ns, tm=tm)
    chunk_v_p = vn_p.reshape(nb_p, seq, D_GATE)[:, (seq - 1) // CHUNK * CHUNK:]
    chunk_v_s = vn_s.reshape(nb_s, ns, D_GATE)

    def channel(xp, xs, i):
        args = (g_ffn, w1b, w2b, g_ple, wpib, wpgb, b_ple_gate)
        return (_ffn_ple(xp, pp, i, *args, tm=FFN_ROW_TILE, tf=FFN_COL_TILE),
                _ffn_ple(xs, ps, i, *args, tm=tm, tf=FFN_COL_TILE))

    xp, xs = channel(xp, xs, 0)

    wb_in, wb_out = bf(w_b_in), bf(w_b_out)
    conv_args = (g_mix[1], wb_in, w_b_dw, b_b_dw, g_b_ln, b_b_ln, wb_out)
    xp, conv_p = _mixer_b_prompt(xp, *conv_args, seq=seq, tm=tm)
    state_t = state_conv.swapaxes(0, 1)
    xs_sb, c_sb = _mixer_b_sample(to_sb(xs), state_t, *conv_args, nb=nb_s, ns=ns)
    xs = from_sb(xs_sb)
    conv_s = jnp.concatenate([state_t[ns:], c_sb.reshape(ns, nb_s, D_MODEL)], axis=0).swapaxes(0, 1)
    xp, xs = channel(xp, xs, 1)

    wc_o = bf(w_c_o)
    wc = w_c_qkv.reshape(D_MODEL, 3, D_MODEL)
    wc_kv_t = bf(wc[:, 1:].transpose(1, 2, 0))
    zeros = jnp.zeros((D_MODEL,), F32)
    ones_rows = jnp.ones((1, 3, 1, D_MODEL), F32)
    bias_rows = jnp.stack([b_c_q.reshape(-1), b_c_k.reshape(-1), zeros]).reshape(1, 3, 1, D_MODEL)
    bias_cols = _col(jnp.stack([b_c_k.reshape(-1), zeros]))
    gain_cols = jnp.ones((2, HEAD_DIM, LANES), F32)
    (qp,) = _proj_rows(xp, g_mix[2], bf(wc[:, 0]), bias_rows[:, :1], ones_rows[:, :1], n_groups=1,
                       has_bias=(True,), normed=(False,), scales=(ATTN_SCALE * LOG2E,), dtypes=(BF16,), tm=tm)
    flat = pl.BlockSpec((None, D_MODEL, tm), lambda i: (i // (seq // tm), 0, i % (seq // tm)))
    blocked = pl.BlockSpec((tm // SB_BLOCK, D_MODEL, SB_BLOCK), lambda i: (i, 0, 0))
    sb_kt_p, sb_vt_p, ktb, vtb = _proj_cols(
        xp, g_mix[2], wc_kv_t, bias_cols, gain_cols, has_bias=(True, False), normed=(False, False), tm=tm,
        grid=(mp // tm,), x_map=lambda i: (i, 0), out_specs=[flat, flat, blocked, blocked],
        out_shape=[jax.ShapeDtypeStruct((nb_p, D_MODEL, seq), F32)] * 2
        + [jax.ShapeDtypeStruct((mp // SB_BLOCK, D_MODEL, SB_BLOCK), BF16)] * 2)
    xp = _proj_res(xp, _sb_prompt(qp[0], ktb, vtb, nb=nb_p, seq=seq), wc_o, tm=tm)
    qs, ks_new, vs_new = _proj_rows(xs, g_mix[2], bf(w_c_qkv), bias_rows, ones_rows, n_groups=1,
                                    has_bias=(True, True, False), normed=(False,) * 3,
                                    scales=(ATTN_SCALE, 1.0, 1.0), dtypes=(F32,) * 3, tm=tm)
    step_blocks = pl.BlockSpec((ns, D_MODEL, LANES), lambda i: (0, 0, 0))
    sb_kt_s, sb_vt_s = _proj_cols(
        to_sb(xs), g_mix[2], wc_kv_t, bias_cols, gain_cols, has_bias=(True, False), normed=(False, False), tm=tm,
        grid=(1,), x_map=lambda i: (0, 0), out_specs=[step_blocks] * 2,
        out_shape=[jax.ShapeDtypeStruct((ns, D_MODEL, nb_s), F32)] * 2)
    tok = lambda a: a.reshape(nb_s, ns, D_MODEL)
    o_s = _sb_sample(tok(qs[0]), tok(ks_new[0]), tok(vs_new[0]), _positions_minor(cache_sb_k),
                     _positions_minor(cache_sb_v), page_table, nb=nb_s, ns=ns)
    xs = _proj_res(xs, o_s.reshape(ms, D_MODEL), wc_o, tm=tm)
    xp, xs = channel(xp, xs, 2)

    wd_qkv, wd_o = bf(w_d_qkv), bf(w_d_o)
    wd = w_d_qkv.reshape(D_MODEL, N_GROUPS_D, 3, D_MODEL)
    tile_heads = lambda g_: jnp.tile(g_, (1, N_HEADS))
    gain_rows = jnp.stack([tile_heads(g_d_q), tile_heads(g_d_k), jnp.ones((N_GROUPS_D, D_MODEL), F32)],
                          axis=1).reshape(N_GROUPS_D, 3, 1, D_MODEL)
    zero_rows = jnp.zeros((N_GROUPS_D, 3, 1, D_MODEL), F32)
    rows_args = dict(n_groups=N_GROUPS_D, has_bias=(False,) * 3, normed=(True, True, False),
                     scales=(ATTN_SCALE, 1.0, 1.0), tm=tm)
    qkv_p = _proj_rows_dsw(xp, g_mix[3], wd_qkv, gain_rows[:, :2], nb=nb_p, seq=seq, tm=tm)
    outs, lses = zip(*[_dsw_prompt(*qkv_p[g], dil, nb=nb_p, seq=seq) for g, (_, dil) in enumerate(DSW_PATTERNS)])
    xp_attn = _merge_proj(xp, outs, lses, wd_o, seq=seq, tm=tm)
    qs, ks_new, vs_new = _proj_rows(xs, g_mix[3], wd_qkv, zero_rows, gain_rows, dtypes=(F32,) * 3, **rows_args)
    xs_sb = to_sb(xs)
    rows_p, rows_s = [], []
    for g, (win, _) in enumerate(DSW_PATTERNS):
        wt = bf(wd[:, g, 1:].transpose(1, 2, 0))
        gain = jnp.stack([jnp.broadcast_to(g_d_k[g][:, None], (HEAD_DIM, LANES)), jnp.ones((HEAD_DIM, LANES), F32)])
        no_bias = jnp.zeros((2, D_MODEL, LANES), F32)
        keep = min(win, seq)
        tw = min(tm, keep)
        first = (seq - keep) // tw
        kt, vt = _proj_cols(
            xp, g_mix[3], wt, no_bias, gain, has_bias=(False, False), normed=(True, False), tm=tw,
            grid=(nb_p, keep // tw), x_map=lambda b, j, first=first, per=seq // tw: (b * per + first + j, 0),
            out_specs=[pl.BlockSpec((None, D_MODEL, tw), lambda b, j: (b, 0, j))] * 2,
            out_shape=[jax.ShapeDtypeStruct((nb_p, D_MODEL, keep), F32)] * 2)
        rows_p += [_positions_major(kt), _positions_major(vt)]
        kt, vt = _proj_cols(
            xs_sb, g_mix[3], wt, no_bias, gain, has_bias=(False, False), normed=(True, False), tm=tm,
            grid=(1,), x_map=lambda i: (0, 0), out_specs=[step_blocks] * 2,
            out_shape=[jax.ShapeDtypeStruct((ns, D_MODEL, nb_s), F32)] * 2)
        rows_s += [_steps_minor(kt, nb_s), _steps_minor(vt, nb_s)]
    tok3 = lambda a: a.reshape(N_GROUPS_D, nb_s, ns, D_MODEL)
    buffers_t = [_positions_minor(b) for b in
                 (cache_dsw0_k, cache_dsw0_v, cache_dsw1_k, cache_dsw1_v, cache_dsw2_k, cache_dsw2_v)]
    o_s = _dsw_sample(tok3(qs), tok3(ks_new), tok3(vs_new), buffers_t, nb=nb_s, ns=ns)
    xs = _proj_res(xs, o_s.reshape(ms, D_MODEL), wd_o, tm=tm)
    xp, xs = channel(xp_attn, xs, 3)

    return (xp.reshape(nb_p, seq, D_MODEL), xs.reshape(nb_s, ns, D_MODEL), chunk_v_p, chunk_v_s,
            conv_p, conv_s,
            _positions_major(sb_kt_p), _positions_major(sb_vt_p),
            _steps_minor(sb_kt_s, nb_s), _steps_minor(sb_vt_s, nb_s),
            *rows_p, *rows_s)
```

```python
import functools
import math

import jax
import jax.numpy as jnp
from jax import lax
from jax.experimental import pallas as pl
from jax.experimental.pallas import tpu as pltpu

F32 = jnp.float32
BF16 = jnp.bfloat16

D_MODEL = 1024
D_FF = 4 * D_MODEL
PLE_DIM = 256
EPS = 1e-6
CHUNK = 128
D_GATE = 2 * D_MODEL
N_GROUPS_A = 8
CONV_WIDTH = 31
CONV_PAST = CONV_WIDTH - 1
HEAD_DIM = 64
N_HEADS = D_MODEL // HEAD_DIM
Q_BLOCK = 128
PAGE_SIZE = 128
ATTN_SCALE = HEAD_DIM ** -0.5
DSW_PATTERNS = ((128, 1), (512, 4), (2048, 16))
N_GROUPS_D = len(DSW_PATTERNS)
NEG_BIG = -1e30
LANES = 128
SUBLANES = 8
MXU_TILE = 256

V7X_VMEM_BYTES = 64 * 1024 * 1024
VMEM_LIMIT = V7X_VMEM_BYTES - 8 * 1024 * 1024
ROW_TILE = 512
FFN_ROW_TILE = 1024
FFN_COL_TILE = 1024
CONV_HALO = 32
CONV_ROWS = 64
SB_BLOCK = 256
SB_GROUP = 4
SB_PAGES_PER_STEP = 8
LOG2E = math.log2(math.e)
DSW_KEY_CHUNK = 1024


def _params(*sem):
    return pltpu.CompilerParams(dimension_semantics=sem, vmem_limit_bytes=VMEM_LIMIT)


def _const_spec(shape):
    nd = len(shape)
    return pl.BlockSpec(shape, lambda *_: (0,) * nd, pipeline_mode=pl.Buffered(1))


def _dot(a, b):
    return jnp.dot(a, b, preferred_element_type=F32)


def _dot_nt(a, b):
    return lax.dot_general(a, b, (((1,), (1,)), ((), ())), preferred_element_type=F32)


def _rms(x, g):
    return x * lax.rsqrt(jnp.mean(x * x, axis=-1, keepdims=True) + EPS) * g


def _sigmoid(x):
    return 1.0 / (1.0 + jnp.exp(-x))


def _gelu(x):
    return 0.5 * x * (1.0 + jnp.tanh(math.sqrt(2.0 / math.pi) * (x + 0.044715 * (x * x * x))))


def _lane_tile(a, reps):
    return a if reps == 1 else jnp.concatenate([a] * reps, axis=1)


def _ffn_ple_body(x_ref, p_ref, gf_ref, w1_ref, w2_ref, gp_ref, wpi_ref, wpg_ref, bpg_ref,
                  o_ref, xn_ref, acc_ref):
    j = pl.program_id(1)

    @pl.when(j == 0)
    def _():
        x = x_ref[...]
        xn_ref[...] = _rms(x, gf_ref[...]).astype(BF16)
        acc_ref[...] = x

    h = jnp.square(jnp.maximum(_dot(xn_ref[...], w1_ref[...]), 0.0))
    acc_ref[...] += _dot(h.astype(BF16), w2_ref[...])

    @pl.when(j == pl.num_programs(1) - 1)
    def _():
        x2 = acc_ref[...]
        gate = _sigmoid(_dot(_rms(x2, gp_ref[...]).astype(BF16), wpg_ref[...]) + bpg_ref[...])
        o_ref[...] = x2 + gate * _dot(p_ref[...].astype(BF16), wpi_ref[...])


def _ffn_ple(x, p, layer, g_ffn, w1b, w2b, g_ple, wpib, wpgb, b_pg, *, tm, tf):
    m = x.shape[0]
    grid = (m // tm, D_FF // tf)
    row = lambda i, j: (i, 0)
    vec = pl.BlockSpec((1, D_MODEL), lambda i, j: (0, 0))
    return pl.pallas_call(
        _ffn_ple_body,
        grid=grid,
        in_specs=[
            pl.BlockSpec((tm, D_MODEL), row),
            pl.BlockSpec((None, tm, PLE_DIM), lambda i, j: (layer, i, 0)),
            vec,
            pl.BlockSpec((None, D_MODEL, tf), lambda i, j: (layer, 0, j)),
            pl.BlockSpec((None, tf, D_MODEL), lambda i, j: (layer, j, 0)),
            vec,
            pl.BlockSpec((None, PLE_DIM, D_MODEL), lambda i, j: (layer, 0, 0)),
            pl.BlockSpec((None, D_MODEL, D_MODEL), lambda i, j: (layer, 0, 0)),
            vec,
        ],
        out_specs=pl.BlockSpec((tm, D_MODEL), row),
        out_shape=jax.ShapeDtypeStruct((m, D_MODEL), F32),
        scratch_shapes=[pltpu.VMEM((tm, D_MODEL), BF16), pltpu.VMEM((tm, D_MODEL), F32)],
        compiler_params=_params("parallel", "arbitrary"),
        name="ffn_ple",
    )(x, p, g_ffn[layer][None], w1b, w2b, g_ple[layer][None], wpib, wpgb, b_pg[layer][None])


def _mixer_a_body(x_ref, g_ref, win_ref, gv_ref, s_ref, bs_ref, wout_ref, o_ref, vn_ref, gated_ref):
    x = x_ref[...]
    h = _rms(x, g_ref[...]).astype(BF16)
    v = _gelu(_dot(h, win_ref[:, D_GATE:]))
    vn = v * lax.rsqrt(jnp.mean(v * v, axis=-1, keepdims=True) + EPS) * gv_ref[...]
    vn_ref[...] = vn
    cg = D_GATE // N_GROUPS_A
    for g in range(N_GROUPS_A):
        cols = slice(g * cg, (g + 1) * cg)
        u = _gelu(_dot(h, win_ref[:, cols]))
        mixed = _dot(s_ref[g], vn_ref[:, cols].astype(BF16)) + _lane_tile(bs_ref[g], cg // LANES)
        gated_ref[:, cols] = (u * mixed).astype(BF16)
    o_ref[...] = x + _dot(gated_ref[...], wout_ref[...])


def _mixer_a(x, g_mix, w_in_b, g_v, w_s, b_s, w_out_b, *, chunk, tm):
    m = x.shape[0]
    reps = tm // chunk
    pos = jnp.arange(tm)
    same = (pos[:, None] // chunk) == (pos[None, :] // chunk)
    causal = pos[None, :] <= pos[:, None]
    w_row = jnp.concatenate([w_s[:, :chunk, :chunk]] * reps, axis=2)
    s_mat = jnp.where((same & causal)[None], jnp.concatenate([w_row] * reps, axis=1), 0.0).astype(BF16)
    bias = jnp.broadcast_to(jnp.tile(b_s[:, :chunk], (1, reps))[:, :, None], (N_GROUPS_A, tm, LANES))
    row = lambda i: (i, 0)
    return pl.pallas_call(
        _mixer_a_body,
        grid=(m // tm,),
        in_specs=[
            pl.BlockSpec((tm, D_MODEL), row),
            _const_spec((1, D_MODEL)),
            _const_spec((D_MODEL, 2 * D_GATE)),
            _const_spec((1, D_GATE)),
            _const_spec((N_GROUPS_A, tm, tm)),
            _const_spec((N_GROUPS_A, tm, LANES)),
            _const_spec((D_GATE, D_MODEL)),
        ],
        out_specs=[pl.BlockSpec((tm, D_MODEL), row), pl.BlockSpec((tm, D_GATE), row)],
        out_shape=[jax.ShapeDtypeStruct((m, D_MODEL), F32), jax.ShapeDtypeStruct((m, D_GATE), F32)],
        scratch_shapes=[pltpu.VMEM((tm, D_GATE), BF16)],
        compiler_params=_params("parallel"),
        name="mixer_a",
    )(x, g_mix[None], w_in_b, g_v[None], s_mat, bias, w_out_b)


def _glu(h, win_ref):
    return _dot(h, win_ref[:, :D_MODEL]) * _sigmoid(_dot(h, win_ref[:, D_MODEL:]))


def _ln_silu(y, g, b):
    mu = jnp.mean(y, axis=-1, keepdims=True)
    yc = y - mu
    z = yc * lax.rsqrt(jnp.mean(yc * yc, axis=-1, keepdims=True) + EPS) * g + b
    return z * _sigmoid(z)


def _mixer_b_prompt_body(x_ref, g_ref, win_ref, wdw_ref, bdw_ref, gln_ref, bln_ref, wout_ref,
                         o_ref, conv_ref, ext_ref, shift_ref, y_ref, act_ref, *, tm, tiles_per_seq):
    i = pl.program_id(0)
    x = x_ref[...]
    c = _glu(_rms(x, g_ref[...]).astype(BF16), win_ref)

    @pl.when(i % tiles_per_seq == 0)
    def _():
        ext_ref[0:CONV_HALO, :] = jnp.zeros((CONV_HALO, D_MODEL), F32)

    @pl.when(i % tiles_per_seq != 0)
    def _():
        ext_ref[0:CONV_HALO, :] = ext_ref[tm:tm + CONV_HALO, :]

    ext_ref[CONV_HALO:CONV_HALO + tm, :] = c
    conv_ref[...] = c[tm - CONV_PAST:, :]
    span = shift_ref.shape[1]
    for r in range(1, SUBLANES):
        shift_ref[r - 1] = ext_ref[r:r + span, :]
    base = CONV_HALO - CONV_PAST
    for c in range(D_MODEL // LANES):
        lanes = slice(c * LANES, (c + 1) * LANES)
        taps = [wdw_ref[j:j + 1, lanes] for j in range(CONV_WIDTH)]
        for r0 in range(0, tm, CONV_ROWS):
            acc = jnp.broadcast_to(bdw_ref[:, lanes], (CONV_ROWS, LANES))
            for j in range(CONV_WIDTH):
                r = (base + j) % SUBLANES
                lo = base + j - r + r0
                rows = ext_ref[lo:lo + CONV_ROWS, lanes] if r == 0 else shift_ref[r - 1, lo:lo + CONV_ROWS, lanes]
                acc = acc + taps[j] * rows
            y_ref[r0:r0 + CONV_ROWS, lanes] = acc
    for r0 in range(0, tm, CONV_ROWS):
        act_ref[r0:r0 + CONV_ROWS, :] = _ln_silu(y_ref[r0:r0 + CONV_ROWS, :], gln_ref[...], bln_ref[...]).astype(BF16)
    o_ref[...] = x + _dot(act_ref[...], wout_ref[...])


def _mixer_b_prompt(x, g_mix, w_in_b, w_dw, b_dw, g_ln, b_ln, w_out_b, *, seq, tm):
    m = x.shape[0]
    nb = m // seq
    tiles_per_seq = seq // tm
    row = lambda i: (i, 0)
    vec = _const_spec((1, D_MODEL))
    return pl.pallas_call(
        functools.partial(_mixer_b_prompt_body, tm=tm, tiles_per_seq=tiles_per_seq),
        grid=(m // tm,),
        in_specs=[
            pl.BlockSpec((tm, D_MODEL), row),
            vec,
            _const_spec((D_MODEL, 2 * D_MODEL)),
            _const_spec((CONV_WIDTH, D_MODEL)),
            vec, vec, vec,
            _const_spec((D_MODEL, D_MODEL)),
        ],
        out_specs=[pl.BlockSpec((tm, D_MODEL), row),
                   pl.BlockSpec((None, CONV_PAST, D_MODEL), lambda i: (i // tiles_per_seq, 0, 0))],
        out_shape=[jax.ShapeDtypeStruct((m, D_MODEL), F32),
                   jax.ShapeDtypeStruct((nb, CONV_PAST, D_MODEL), F32)],
        scratch_shapes=[pltpu.VMEM((tm + CONV_HALO, D_MODEL), F32),
                        pltpu.VMEM((SUBLANES - 1, tm + CONV_HALO - SUBLANES, D_MODEL), F32),
                        pltpu.VMEM((tm, D_MODEL), F32), pltpu.VMEM((tm, D_MODEL), BF16)],
        compiler_params=_params("arbitrary"),
        name="mixer_b_prompt",
    )(x, g_mix[None], w_in_b, w_dw, b_dw[None], g_ln[None], b_ln[None], w_out_b)


def _mixer_b_sample_body(x_ref, st_ref, g_ref, win_ref, wdw_ref, bdw_ref, gln_ref, bln_ref, wout_ref,
                         o_ref, c_ref, act_ref, *, nb, ns):
    x = x_ref[...]
    c_ref[...] = _glu(_rms(x, g_ref[...]).astype(BF16), win_ref)
    for s in range(ns):
        acc = jnp.broadcast_to(bdw_ref[...], (nb, D_MODEL))
        for j in range(CONV_WIDTH):
            k = s + j
            src = st_ref[k] if k < CONV_PAST else c_ref[(k - CONV_PAST) * nb:(k - CONV_PAST + 1) * nb, :]
            acc = acc + wdw_ref[j:j + 1, :] * src
        act_ref[s * nb:(s + 1) * nb, :] = _ln_silu(acc, gln_ref[...], bln_ref[...]).astype(BF16)
    o_ref[...] = x + _dot(act_ref[...], wout_ref[...])


def _mixer_b_sample(x_sb, state_t, g_mix, w_in_b, w_dw, b_dw, g_ln, b_ln, w_out_b, *, nb, ns):
    m = nb * ns
    vec = _const_spec((1, D_MODEL))
    full = _const_spec((m, D_MODEL))
    return pl.pallas_call(
        functools.partial(_mixer_b_sample_body, nb=nb, ns=ns),
        grid=(1,),
        in_specs=[full, _const_spec((CONV_PAST, nb, D_MODEL)), vec,
                  _const_spec((D_MODEL, 2 * D_MODEL)), _const_spec((CONV_WIDTH, D_MODEL)),
                  vec, vec, vec, _const_spec((D_MODEL, D_MODEL))],
        out_specs=[pl.BlockSpec((m, D_MODEL), lambda i: (0, 0)), pl.BlockSpec((m, D_MODEL), lambda i: (0, 0))],
        out_shape=[jax.ShapeDtypeStruct((m, D_MODEL), F32), jax.ShapeDtypeStruct((m, D_MODEL), F32)],
        scratch_shapes=[pltpu.VMEM((m, D_MODEL), BF16)],
        compiler_params=_params("arbitrary"),
        name="mixer_b_sample",
    )(x_sb, state_t, g_mix[None], w_in_b, w_dw, b_dw[None], g_ln[None], b_ln[None], w_out_b)


def _proj_res_body(x_ref, a_ref, w_ref, o_ref):
    o_ref[...] = x_ref[...] + _dot(a_ref[...].astype(BF16), w_ref[...])


def _proj_res(x, a, w_b, *, tm):
    m = x.shape[0]
    row = lambda i: (i, 0)
    return pl.pallas_call(
        _proj_res_body,
        grid=(m // tm,),
        in_specs=[pl.BlockSpec((tm, D_MODEL), row), pl.BlockSpec((tm, D_MODEL), row),
                  _const_spec((D_MODEL, D_MODEL))],
        out_specs=pl.BlockSpec((tm, D_MODEL), row),
        out_shape=jax.ShapeDtypeStruct((m, D_MODEL), F32),
        compiler_params=_params("parallel"),
        name="proj_res",
    )(x, a, w_b)


def _head_mean_matrix():
    lane_head = jnp.arange(MXU_TILE) // HEAD_DIM
    return jnp.where(lane_head[:, None] == lane_head[None, :], 1.0 / HEAD_DIM, 0.0).astype(BF16)


def _head_norm(y, hs_ref, gain):
    y2 = (y * y).astype(BF16)
    ms = jnp.concatenate([_dot(y2[:, c * MXU_TILE:(c + 1) * MXU_TILE], hs_ref[...])
                          for c in range(D_MODEL // MXU_TILE)], axis=1)
    return y * lax.rsqrt(ms + EPS) * gain


def _proj_rows_body(x_ref, g_ref, w_ref, bias_ref, gain_ref, hs_ref, *rest, has_bias, normed, scales):
    outs, h_ref = rest[:-1], rest[-1]

    @pl.when(pl.program_id(1) == 0)
    def _():
        h_ref[...] = _rms(x_ref[...], g_ref[...]).astype(BF16)

    h = h_ref[...]
    for j, out in enumerate(outs):
        y = _dot(h, w_ref[:, j * D_MODEL:(j + 1) * D_MODEL])
        if has_bias[j]:
            y = y + bias_ref[j]
        if normed[j]:
            y = _head_norm(y, hs_ref, gain_ref[j])
        out[...] = (y * scales[j]).astype(out.dtype)


def _proj_rows(x, g_mix, w_b, bias, gain, *, n_groups, has_bias, normed, scales, dtypes, tm):
    m = x.shape[0]
    n_out = len(dtypes)
    per_group = pl.BlockSpec((None, n_out, 1, D_MODEL), lambda i, g: (g, 0, 0, 0))
    out = pl.BlockSpec((None, tm, D_MODEL), lambda i, g: (g, i, 0))
    return pl.pallas_call(
        functools.partial(_proj_rows_body, has_bias=has_bias, normed=normed, scales=scales),
        grid=(m // tm, n_groups),
        in_specs=[pl.BlockSpec((tm, D_MODEL), lambda i, g: (i, 0)),
                  pl.BlockSpec((1, D_MODEL), lambda i, g: (0, 0)),
                  pl.BlockSpec((D_MODEL, n_out * D_MODEL), lambda i, g: (0, g)),
                  per_group, per_group,
                  pl.BlockSpec((MXU_TILE, MXU_TILE), lambda i, g: (0, 0))],
        out_specs=[out] * n_out,
        out_shape=[jax.ShapeDtypeStruct((n_groups, m, D_MODEL), dt) for dt in dtypes],
        scratch_shapes=[pltpu.VMEM((tm, D_MODEL), BF16)],
        compiler_params=_params("parallel", "arbitrary"),
        name="proj_rows",
    )(x, g_mix[None], w_b, bias, gain, _head_mean_matrix())


def _store_cols(ref, yt):
    if len(ref.shape) == 2:
        ref[...] = yt.astype(ref.dtype)
    else:
        width = ref.shape[2]
        for c in range(ref.shape[0]):
            ref[c] = yt[:, c * width:(c + 1) * width].astype(ref.dtype)


def _proj_cols_body(x_ref, g_ref, wt_ref, bias_ref, gain_ref, *outs, n_out, has_bias, normed):
    h = _rms(x_ref[...], g_ref[...]).astype(BF16)
    reps = h.shape[0] // LANES
    for j in range(n_out):
        yt = _dot_nt(wt_ref[j], h)
        if has_bias[j]:
            yt = yt + _lane_tile(bias_ref[j], reps)
        if normed[j]:
            gain = _lane_tile(gain_ref[j], reps)
            heads = []
            for hh in range(N_HEADS):
                blk = yt[hh * HEAD_DIM:(hh + 1) * HEAD_DIM, :]
                heads.append(blk * lax.rsqrt(jnp.mean(blk * blk, axis=0, keepdims=True) + EPS) * gain)
            yt = jnp.concatenate(heads, axis=0)
        for out in outs[j::n_out]:
            _store_cols(out, yt)


def _proj_cols(x, g_mix, wt_b, bias_col, gain_col, *, has_bias, normed, tm, grid, x_map, out_specs, out_shape):
    n_out = wt_b.shape[0]
    return pl.pallas_call(
        functools.partial(_proj_cols_body, n_out=n_out, has_bias=has_bias, normed=normed),
        grid=grid,
        in_specs=[pl.BlockSpec((tm, D_MODEL), x_map),
                  _const_spec((1, D_MODEL)), _const_spec(wt_b.shape),
                  _const_spec(bias_col.shape), _const_spec(gain_col.shape)],
        out_specs=out_specs,
        out_shape=out_shape,
        compiler_params=_params(*(["parallel"] * len(grid))),
        name="proj_cols",
    )(x, g_mix[None], wt_b, bias_col, gain_col)


def _col(v):
    return jnp.broadcast_to(v[:, :, None], v.shape + (LANES,))


def _sb_block(z, tri, valid):
    n = z.shape[1]
    log_keep = -(jnp.maximum(z, 0.0) + jnp.log(1.0 + jnp.exp(-jnp.abs(z))))
    if valid is not None:
        log_keep = jnp.where(valid, log_keep, 0.0)
    hi = log_keep.astype(BF16)
    lo = (log_keep - hi.astype(F32)).astype(BF16)
    sums = _dot(hi, tri) + _dot(lo, tri)
    return z + log_keep + sums[:, :n], sums[:, n:]


def _sb_tri(n):
    j = jnp.arange(n)
    return jnp.concatenate([(j[:, None] > j[None, :]), jnp.ones((n, n), bool)], axis=1).astype(BF16)


def _sb_prompt_body(q_ref, kt_ref, vt_ref, tri_ref, o_ref, acc_ref, carry_ref):
    qb = pl.program_id(2)
    heads = q_ref.shape[1] // HEAD_DIM
    tri = tri_ref[...]
    acc_ref[...] = jnp.zeros(acc_ref.shape, F32)
    carry_ref[...] = jnp.zeros(carry_ref.shape, F32)
    row = lax.broadcasted_iota(jnp.int32, (SB_BLOCK, SB_BLOCK), 0)
    col = lax.broadcasted_iota(jnp.int32, (SB_BLOCK, SB_BLOCK), 1)
    strictly_before = col < row

    def sweep(kbs, diagonal_first):
        outs, carries = [], []
        for hh in range(heads):
            sl = slice(hh * HEAD_DIM, (hh + 1) * HEAD_DIM)
            q = q_ref[:, sl]
            parts = []
            for n, kb in enumerate(kbs):
                valid = strictly_before if (diagonal_first and n == 0) else None
                z = _dot(q, kt_ref[kb, sl, :])
                sp = jnp.maximum(z, 0.0) + jnp.log2(1.0 + jnp.exp2(-jnp.abs(z)))
                if valid is not None:
                    sp = jnp.where(valid, sp, 0.0)
                after = _dot(sp.astype(BF16), tri)
                parts.append((kb, valid, z - sp - after, jnp.sum(sp, axis=-1, keepdims=True)))
            carry = carry_ref[hh]
            o = acc_ref[:, sl]
            for kb, valid, t, total in parts:
                a = jnp.exp2(t - _lane_tile(carry, SB_BLOCK // LANES))
                if valid is not None:
                    a = jnp.where(valid, a, 0.0)
                o = o + _dot_nt(a.astype(BF16), vt_ref[kb, sl, :])
                carry = carry + total
            outs.append(o)
            carries.append(carry)
        acc_ref[...] = jnp.concatenate(outs, axis=1)
        carry_ref[...] = jnp.stack(carries)

    rest = qb % SB_GROUP
    for r in range(SB_GROUP):
        @pl.when(rest == r)
        def _(r=r):
            sweep([qb - n for n in range(r + 1)], True)

    first = qb - rest

    def group(i, _):
        kb = first - 1 - SB_GROUP * i
        sweep([kb - n for n in range(SB_GROUP)], False)
        return 0

    lax.fori_loop(0, qb // SB_GROUP, group, 0)
    o_ref[...] = acc_ref[...]


def _sb_prompt(qs_b, kt_b, vt_b, *, nb, seq):
    lanes = 2 * HEAD_DIM
    nq = seq // SB_BLOCK
    kv = pl.BlockSpec((nq, lanes, SB_BLOCK), lambda b, hp, qb: (b, hp, 0))
    qo = lambda b, hp, qb: (b * nq + qb, hp)
    j = jnp.arange(SB_BLOCK)
    tri = (j[:, None] > j[None, :]).astype(BF16)
    return pl.pallas_call(
        _sb_prompt_body,
        grid=(nb, D_MODEL // lanes, nq),
        in_specs=[pl.BlockSpec((SB_BLOCK, lanes), qo), kv, kv, _const_spec((SB_BLOCK, SB_BLOCK))],
        out_specs=pl.BlockSpec((SB_BLOCK, lanes), qo),
        out_shape=jax.ShapeDtypeStruct((nb * seq, D_MODEL), F32),
        scratch_shapes=[pltpu.VMEM((SB_BLOCK, lanes), F32), pltpu.VMEM((2, SB_BLOCK, LANES), F32)],
        compiler_params=_params("parallel", "parallel", "arbitrary"),
        name="sb_prompt",
    )(qs_b, kt_b, vt_b, tri)


def _head_rows(q):
    ns = q.shape[0]
    rows = jnp.concatenate([jnp.broadcast_to(q[s:s + 1, :], (N_HEADS, D_MODEL)) for s in range(ns)], axis=0)
    r = lax.broadcasted_iota(jnp.int32, rows.shape, 0)
    lane = lax.broadcasted_iota(jnp.int32, rows.shape, 1)
    return jnp.where((lane >> 6) == (r & (N_HEADS - 1)), rows, 0.0).astype(BF16)


def _own_head(full, ns):
    r = lax.broadcasted_iota(jnp.int32, (N_HEADS, D_MODEL), 0)
    lane = lax.broadcasted_iota(jnp.int32, (N_HEADS, D_MODEL), 1)
    own = (lane >> 6) == r
    return [jnp.sum(jnp.where(own, full[s * N_HEADS:(s + 1) * N_HEADS, :], 0.0), axis=0, keepdims=True)
            for s in range(ns)]


def _new_rows(slab_ref, new_ref, ns):
    slab_ref[...] = jnp.zeros(slab_ref.shape, F32)
    slab_ref[0:ns, :] = new_ref[...]


def _sb_sample_body(pt_ref, q_ref, kn_ref, vn_ref, tri_ref, *rest, ns, pages):
    kt_refs, vt_refs = rest[:pages], rest[pages:2 * pages]
    o_ref, qh_ref, acc_ref, carry_ref, ks_ref, vs_ref = rest[2 * pages:]
    c = pl.program_id(1)
    rows = ns * N_HEADS
    tri = tri_ref[...]

    @pl.when(c == 0)
    def _():
        qh_ref[...] = _head_rows(q_ref[...])
        _new_rows(ks_ref, kn_ref, ns)
        _new_rows(vs_ref, vn_ref, ns)
        r = lax.broadcasted_iota(jnp.int32, (rows, PAGE_SIZE), 0)
        col = lax.broadcasted_iota(jnp.int32, (rows, PAGE_SIZE), 1)
        valid = col < (r >> 4)
        t, total = _sb_block(_dot_nt(qh_ref[...], ks_ref[...].astype(BF16)), tri, valid)
        a = jnp.where(valid, jnp.exp(t), 0.0)
        acc_ref[...] = _dot(a.astype(BF16), vs_ref[...].astype(BF16))
        carry_ref[...] = total

    qh = qh_ref[...]
    fronts = [_sb_block(_dot(qh, kt_refs[p][...].astype(BF16)), tri, None) for p in range(pages)]
    carry = carry_ref[...]
    acc = acc_ref[...]
    for p, (t, total) in enumerate(fronts):
        acc = acc + _dot_nt(jnp.exp(t + carry).astype(BF16), vt_refs[p][...].astype(BF16))
        carry = carry + total
    acc_ref[...] = acc
    carry_ref[...] = carry

    @pl.when(c == pl.num_programs(1) - 1)
    def _():
        out = _own_head(acc_ref[...], ns)
        for s in range(ns):
            o_ref[s:s + 1, :] = out[s]


def _sb_sample(q_s, k_new, v_new, cache_kt, cache_vt, page_table, *, nb, ns):
    n_pages = page_table.shape[1]
    pages = SB_PAGES_PER_STEP
    steps = n_pages // pages
    rows = ns * N_HEADS
    tok = pl.BlockSpec((None, ns, D_MODEL), lambda b, c, pt: (b, 0, 0))

    def page_spec(p):
        return pl.BlockSpec((None, D_MODEL, PAGE_SIZE),
                            lambda b, c, pt: (pt[b * n_pages + n_pages - 1 - (c * pages + p)], 0, 0))

    grid_spec = pltpu.PrefetchScalarGridSpec(
        num_scalar_prefetch=1,
        grid=(nb, steps),
        in_specs=[tok, tok, tok, pl.BlockSpec((PAGE_SIZE, 2 * PAGE_SIZE), lambda b, c, pt: (0, 0))]
        + [page_spec(p) for p in range(pages)] * 2,
        out_specs=tok,
        scratch_shapes=[pltpu.VMEM((rows, D_MODEL), BF16), pltpu.VMEM((rows, D_MODEL), F32),
                        pltpu.VMEM((rows, PAGE_SIZE), F32),
                        pltpu.VMEM((PAGE_SIZE, D_MODEL), F32), pltpu.VMEM((PAGE_SIZE, D_MODEL), F32)],
    )
    return pl.pallas_call(
        functools.partial(_sb_sample_body, ns=ns, pages=pages),
        grid_spec=grid_spec,
        out_shape=jax.ShapeDtypeStruct((nb, ns, D_MODEL), F32),
        compiler_params=_params("parallel", "arbitrary"),
        name="sb_sample",
    )(page_table.reshape(-1), q_s, k_new, v_new, _sb_tri(PAGE_SIZE), *([cache_kt] * pages), *([cache_vt] * pages))


def _proj_rows_dsw_body(x_ref, g_ref, w_ref, gain_ref, hs_ref, *rest, dils):
    outs, y_ref = rest[:-1], rest[-1]
    h = _rms(x_ref[...], g_ref[...]).astype(BF16)
    tm = h.shape[0]
    for g, dil in enumerate(dils):
        for j in range(3):
            col = (3 * g + j) * D_MODEL
            y = _dot(h, w_ref[:, col:col + D_MODEL])
            if j < 2:
                y = _head_norm(y, hs_ref, gain_ref[g, j])
            if j == 0:
                y = y * ATTN_SCALE
            out = outs[3 * g + j]
            if dil == 1:
                out[0] = y.astype(BF16)
            else:
                for c in range(D_MODEL // LANES):
                    y_ref[c] = y[:, c * LANES:(c + 1) * LANES]
                for r in range(dil):
                    rows = [y_ref[c, pl.ds(r, tm // dil, stride=dil), :] for c in range(D_MODEL // LANES)]
                    out[r] = jnp.concatenate(rows, axis=1).astype(BF16)


def _proj_rows_dsw(x, g_mix, w_b, gain, *, nb, seq, tm):
    m = x.shape[0]
    dils = tuple(d for _, d in DSW_PATTERNS)
    per_seq = seq // tm
    out_specs, out_shape = [], []
    for dil in dils:
        for _ in range(3):
            out_specs.append(pl.BlockSpec((None, dil, tm // dil, D_MODEL),
                                          lambda i: (i // per_seq, 0, i % per_seq, 0)))
            out_shape.append(jax.ShapeDtypeStruct((nb, dil, seq // dil, D_MODEL), BF16))
    outs = pl.pallas_call(
        functools.partial(_proj_rows_dsw_body, dils=dils),
        grid=(m // tm,),
        in_specs=[pl.BlockSpec((tm, D_MODEL), lambda i: (i, 0)), _const_spec((1, D_MODEL)),
                  _const_spec(w_b.shape), _const_spec(gain.shape), _const_spec((MXU_TILE, MXU_TILE))],
        out_specs=out_specs,
        out_shape=out_shape,
        scratch_shapes=[pltpu.VMEM((D_MODEL // LANES, tm, LANES), F32)],
        compiler_params=_params("parallel"),
        name="proj_rows_dsw",
    )(x, g_mix[None], w_b, gain, _head_mean_matrix())
    return [outs[3 * g:3 * g + 3] for g in range(len(dils))]


def _dsw_prompt_body(q_ref, kp_ref, kc_ref, vp_ref, vc_ref, o_ref, l_ref):
    n = pl.program_id(2)
    row = lax.broadcasted_iota(jnp.int32, (Q_BLOCK, Q_BLOCK), 0)
    col = lax.broadcasted_iota(jnp.int32, (Q_BLOCK, Q_BLOCK), 1)
    bias = jnp.concatenate([jnp.where((col >= row) & (n > 0), 0.0, NEG_BIG),
                            jnp.where(col <= row, 0.0, NEG_BIG)], axis=1)
    lanes = 2 * HEAD_DIM
    lane = lax.broadcasted_iota(jnp.int32, (1, lanes), 1)
    first = lane < HEAD_DIM
    keep = [first.astype(BF16), (~first).astype(BF16)]
    outs, lses = [], []
    for hp in range(D_MODEL // lanes):
        cols = slice(hp * lanes, (hp + 1) * lanes)
        q2 = q_ref[:, cols]
        k2 = jnp.concatenate([kp_ref[:, cols], kc_ref[:, cols]], axis=0)
        v2 = jnp.concatenate([vp_ref[:, cols], vc_ref[:, cols]], axis=0)
        o2, l2 = [], []
        for hh in range(2):
            s = _dot_nt(q2 * keep[hh], k2) + bias
            mx = jnp.max(s, axis=-1, keepdims=True)
            p = jnp.exp(s - mx)
            den = jnp.sum(p, axis=-1, keepdims=True)
            o2.append(_dot(p.astype(BF16), v2) / den)
            l2.append(mx + jnp.log(den))
        outs.append(jnp.where(first, o2[0], o2[1]))
        lses.append(jnp.where(first, l2[0], l2[1]))
    o_ref[...] = jnp.concatenate(outs, axis=1)
    l_ref[...] = jnp.concatenate(lses, axis=1)


def _dsw_prompt(qs_b, k_b, v_b, dil, *, nb, seq):
    length = seq // dil
    cur = pl.BlockSpec((None, None, Q_BLOCK, D_MODEL), lambda b, r, n: (b, r, n, 0))
    prev = pl.BlockSpec((None, None, Q_BLOCK, D_MODEL), lambda b, r, n: (b, r, jnp.maximum(n - 1, 0), 0))
    sds = jax.ShapeDtypeStruct((nb, dil, length, D_MODEL), F32)
    return pl.pallas_call(
        _dsw_prompt_body,
        grid=(nb, dil, length // Q_BLOCK),
        in_specs=[cur, prev, cur, prev, cur],
        out_specs=[cur, cur],
        out_shape=[sds, sds],
        compiler_params=_params("parallel", "parallel", "arbitrary"),
        name="dsw_prompt",
    )(qs_b, k_b, k_b, v_b, v_b)


def _merge_proj_body(x_ref, *rest, dils):
    n = len(dils)
    o_refs, l_refs, (w_ref, out_ref, tok_ref) = rest[:n], rest[n:2 * n], rest[2 * n:]
    tm = x_ref.shape[0]

    def token_major(ref, dil, slot):
        if dil == 1:
            return ref[0]
        for r in range(dil):
            blk = ref[r]
            for c in range(D_MODEL // LANES):
                tok_ref[slot, c, pl.ds(r, tm // dil, stride=dil), :] = blk[:, c * LANES:(c + 1) * LANES]
        return jnp.concatenate([tok_ref[slot, c] for c in range(D_MODEL // LANES)], axis=1)

    outs = [token_major(o_refs[g], dil, 2 * g) for g, dil in enumerate(dils)]
    lses = [token_major(l_refs[g], dil, 2 * g + 1) for g, dil in enumerate(dils)]
    mx = functools.reduce(jnp.maximum, lses)
    es = [jnp.exp(l - mx) for l in lses]
    o = sum(e * o for e, o in zip(es, outs)) / sum(es)
    out_ref[...] = x_ref[...] + _dot(o.astype(BF16), w_ref[...])


def _merge_proj(x, outs, lses, w_b, *, seq, tm):
    m = x.shape[0]
    dils = tuple(d for _, d in DSW_PATTERNS)
    per_seq = seq // tm
    row = pl.BlockSpec((tm, D_MODEL), lambda i: (i, 0))
    split = [pl.BlockSpec((None, dil, tm // dil, D_MODEL), lambda i: (i // per_seq, 0, i % per_seq, 0))
             for dil in dils]
    return pl.pallas_call(
        functools.partial(_merge_proj_body, dils=dils),
        grid=(m // tm,),
        in_specs=[row] + split + split + [_const_spec((D_MODEL, D_MODEL))],
        out_specs=row,
        out_shape=jax.ShapeDtypeStruct((m, D_MODEL), F32),
        scratch_shapes=[pltpu.VMEM((2 * len(dils), D_MODEL // LANES, tm, LANES), F32)],
        compiler_params=_params("parallel"),
        name="merge_proj",
    )(x, *outs, *lses, w_b)


def _softmax_accumulate(m_ref, l_ref, acc_ref, s, value_dot):
    m_old = m_ref[...]
    m_new = jnp.maximum(m_old, jnp.max(s, axis=-1, keepdims=True))
    alpha = jnp.exp(m_old - m_new)
    p = jnp.exp(s - m_new[:, 0:1])
    l_ref[...] = alpha * l_ref[...] + jnp.sum(p, axis=-1, keepdims=True)
    acc_ref[...] = alpha[:, 0:1] * acc_ref[...] + value_dot(p.astype(BF16))
    m_ref[...] = m_new


def _dsw_sample_body(q_ref, kn_ref, vn_ref, k0_ref, v0_ref, k1_ref, v1_ref, k2_ref, v2_ref, o_ref,
                     qh_ref, m_ref, l_ref, acc_ref, ks_ref, vs_ref, *, ns):
    c = pl.program_id(1)
    rows = ns * N_HEADS
    step = lax.broadcasted_iota(jnp.int32, (rows, PAGE_SIZE), 0) >> 4
    col = lax.broadcasted_iota(jnp.int32, (rows, PAGE_SIZE), 1)

    def buffer_scores(g, kt_ref):
        n, dil = kt_ref.shape[1], DSW_PATTERNS[g][1]
        pos = lax.broadcasted_iota(jnp.int32, (rows, n), 1)
        qstep = lax.broadcasted_iota(jnp.int32, (rows, n), 0) >> 4
        valid = (pos >= qstep) if dil == 1 else ((pos & (dil - 1)) == qstep)
        return jnp.where(valid, _dot(qh_ref[g], kt_ref[...].astype(BF16)), NEG_BIG)

    def row_max(s):
        return jnp.max(s, axis=-1, keepdims=True)

    def row_sum(p):
        return jnp.sum(p, axis=-1, keepdims=True)

    def set_state(g, mx, den, acc):
        m_ref[g] = jnp.broadcast_to(mx, (rows, LANES))
        l_ref[g] = jnp.broadcast_to(den, (rows, LANES))
        acc_ref[g] = acc

    @pl.when(c == 0)
    def _():
        s_new = []
        for g, (_, dil) in enumerate(DSW_PATTERNS):
            qh_ref[g] = _head_rows(q_ref[g])
            _new_rows(ks_ref.at[g], kn_ref.at[g], ns)
            _new_rows(vs_ref.at[g], vn_ref.at[g], ns)
            new_ok = (col <= step) if dil == 1 else (col == step)
            s_new.append(jnp.where(new_ok, _dot_nt(qh_ref[g], ks_ref[g].astype(BF16)), NEG_BIG))
        for g, (kt_ref, vt_ref) in enumerate(((k0_ref, v0_ref), (k1_ref, v1_ref))):
            s_buf = buffer_scores(g, kt_ref)
            mx = jnp.maximum(row_max(s_new[g]), row_max(s_buf))
            p_new, p_buf = jnp.exp(s_new[g] - mx), jnp.exp(s_buf - mx)
            set_state(g, mx, row_sum(p_new) + row_sum(p_buf),
                      _dot(p_new.astype(BF16), vs_ref[g].astype(BF16))
                      + _dot_nt(p_buf.astype(BF16), vt_ref[...].astype(BF16)))
        mx = row_max(s_new[2])
        p_new = jnp.exp(s_new[2] - mx)
        set_state(2, mx, row_sum(p_new), _dot(p_new.astype(BF16), vs_ref[2].astype(BF16)))

    _softmax_accumulate(m_ref.at[2], l_ref.at[2], acc_ref.at[2], buffer_scores(2, k2_ref),
                        lambda p: _dot_nt(p, v2_ref[...].astype(BF16)))

    @pl.when(c == pl.num_programs(1) - 1)
    def _():
        lses = [m_ref[g][:, 0:1] + jnp.log(l_ref[g][:, 0:1]) for g in range(N_GROUPS_D)]
        mx = jnp.maximum(jnp.maximum(lses[0], lses[1]), lses[2])
        es = [jnp.exp(l - mx) for l in lses]
        merged = sum(es[g] * (acc_ref[g] / l_ref[g][:, 0:1]) for g in range(N_GROUPS_D)) / (es[0] + es[1] + es[2])
        out = _own_head(merged, ns)
        for s in range(ns):
            o_ref[s:s + 1, :] = out[s]


def _dsw_sample(q_s, k_new, v_new, buffers_t, *, nb, ns):
    rows = ns * N_HEADS
    tok = pl.BlockSpec((N_GROUPS_D, None, ns, D_MODEL), lambda b, c: (0, b, 0, 0))
    specs = []
    for g, (win, dil) in enumerate(DSW_PATTERNS):
        assert buffers_t[2 * g].shape[2] == win and win // dil == PAGE_SIZE and (dil == 1 or ns <= dil)
        assert dil & (dil - 1) == 0 and DSW_KEY_CHUNK % dil == 0
    w0, w1, w2 = (w for w, _ in DSW_PATTERNS)
    assert max(w0, w1) <= DSW_KEY_CHUNK and w2 % DSW_KEY_CHUNK == 0
    whole = lambda w: pl.BlockSpec((None, D_MODEL, w), lambda b, c: (b, 0, 0))
    chunk = pl.BlockSpec((None, D_MODEL, DSW_KEY_CHUNK), lambda b, c: (b, 0, c))
    specs = [whole(w0), whole(w0), whole(w1), whole(w1), chunk, chunk]
    state = lambda lanes: pltpu.VMEM((N_GROUPS_D, rows, lanes), F32)
    state_rows = lambda n: pltpu.VMEM((N_GROUPS_D, n, D_MODEL), F32)
    return pl.pallas_call(
        functools.partial(_dsw_sample_body, ns=ns),
        grid=(nb, w2 // DSW_KEY_CHUNK),
        in_specs=[tok, tok, tok] + specs,
        out_specs=pl.BlockSpec((None, ns, D_MODEL), lambda b, c: (b, 0, 0)),
        out_shape=jax.ShapeDtypeStruct((nb, ns, D_MODEL), F32),
        scratch_shapes=[pltpu.VMEM((N_GROUPS_D, rows, D_MODEL), BF16), state(LANES), state(LANES), state(D_MODEL),
                        state_rows(PAGE_SIZE), state_rows(PAGE_SIZE)],
        compiler_params=_params("parallel", "arbitrary"),
        name="dsw_sample",
    )(q_s, k_new, v_new, *buffers_t)


def _positions_minor(a):
    return a.transpose(0, 2, 3, 1).reshape(a.shape[0], D_MODEL, a.shape[1])


def _positions_major(at):
    return at.reshape(at.shape[0], N_HEADS, HEAD_DIM, at.shape[2]).transpose(0, 3, 1, 2)


def _steps_minor(at, nb):
    return at.reshape(at.shape[0], N_HEADS, HEAD_DIM, nb).transpose(3, 0, 1, 2)


def kernel(x_prompt, x_sample, cache_sb_k, cache_sb_v, cache_dsw0_k, cache_dsw0_v, cache_dsw1_k, cache_dsw1_v, cache_dsw2_k, cache_dsw2_v, state_conv, page_table, p_prompt, p_sample, g_mix, g_ffn, g_ple, w_ff1, w_ff2, w_ple_in, w_ple_gate, b_ple_gate, w_a_in, g_a_v, w_a_s, b_a_s, w_a_out, w_b_in, w_b_dw, b_b_dw, g_b_ln, b_b_ln, w_b_out, w_c_qkv, b_c_q, b_c_k, w_c_o, w_d_qkv, g_d_q, g_d_k, w_d_o):
    nb_p, seq, _ = x_prompt.shape
    nb_s, ns, _ = x_sample.shape
    depth = g_mix.shape[0]
    mp, ms = nb_p * seq, nb_s * ns
    tm = ROW_TILE
    assert mp % tm == 0 and ms == tm and seq % tm == 0 and nb_s == LANES and mp % FFN_ROW_TILE == 0

    bf = lambda w: w.astype(BF16)
    w1b, w2b, wpib, wpgb = bf(w_ff1), bf(w_ff2), bf(w_ple_in), bf(w_ple_gate)
    xp = x_prompt.reshape(mp, D_MODEL)
    xs = x_sample.reshape(ms, D_MODEL)
    pp = p_prompt.reshape(depth, mp, PLE_DIM)
    ps = p_sample.reshape(depth, ms, PLE_DIM)
    to_sb = lambda a: a.reshape(nb_s, ns, D_MODEL).swapaxes(0, 1).reshape(ms, D_MODEL)
    from_sb = lambda a: a.reshape(ns, nb_s, D_MODEL).swapaxes(0, 1).reshape(ms, D_MODEL)

    wa_in, wa_out = bf(w_a_in), bf(w_a_out)
    xp, vn_p = _mixer_a(xp, g_mix[0], wa_in, g_a_v, w_a_s, b_a_s, wa_out, chunk=CHUNK, tm=tm)
    xs, vn_s = _mixer_a(xs, g_mix[0], wa_in, g_a_v, w_a_s, b_a_s, wa_out, chunk=ns, tm=tm)
    chunk_v_p = vn_p.reshape(nb_p, seq, D_GATE)[:, (seq - 1) // CHUNK * CHUNK:]
    chunk_v_s = vn_s.reshape(nb_s, ns, D_GATE)

    def channel(xp, xs, i):
        args = (g_ffn, w1b, w2b, g_ple, wpib, wpgb, b_ple_gate)
        return (_ffn_ple(xp, pp, i, *args, tm=FFN_ROW_TILE, tf=FFN_COL_TILE),
                _ffn_ple(xs, ps, i, *args, tm=tm, tf=FFN_COL_TILE))

    xp, xs = channel(xp, xs, 0)

    wb_in, wb_out = bf(w_b_in), bf(w_b_out)
    conv_args = (g_mix[1], wb_in, w_b_dw, b_b_dw, g_b_ln, b_b_ln, wb_out)
    xp, conv_p = _mixer_b_prompt(xp, *conv_args, seq=seq, tm=tm)
    state_t = state_conv.swapaxes(0, 1)
    xs_sb, c_sb = _mixer_b_sample(to_sb(xs), state_t, *conv_args, nb=nb_s, ns=ns)
    xs = from_sb(xs_sb)
    conv_s = jnp.concatenate([state_t[ns:], c_sb.reshape(ns, nb_s, D_MODEL)], axis=0).swapaxes(0, 1)
    xp, xs = channel(xp, xs, 1)

    wc_o = bf(w_c_o)
    wc = w_c_qkv.reshape(D_MODEL, 3, D_MODEL)
    wc_kv_t = bf(wc[:, 1:].transpose(1, 2, 0))
    zeros = jnp.zeros((D_MODEL,), F32)
    ones_rows = jnp.ones((1, 3, 1, D_MODEL), F32)
    bias_rows = jnp.stack([b_c_q.reshape(-1), b_c_k.reshape(-1), zeros]).reshape(1, 3, 1, D_MODEL)
    bias_cols = _col(jnp.stack([b_c_k.reshape(-1), zeros]))
    gain_cols = jnp.ones((2, HEAD_DIM, LANES), F32)
    (qp,) = _proj_rows(xp, g_mix[2], bf(wc[:, 0]), bias_rows[:, :1], ones_rows[:, :1], n_groups=1,
                       has_bias=(True,), normed=(False,), scales=(ATTN_SCALE * LOG2E,), dtypes=(BF16,), tm=tm)
    flat = pl.BlockSpec((None, D_MODEL, tm), lambda i: (i // (seq // tm), 0, i % (seq // tm)))
    blocked = pl.BlockSpec((tm // SB_BLOCK, D_MODEL, SB_BLOCK), lambda i: (i, 0, 0))
    sb_kt_p, sb_vt_p, ktb, vtb = _proj_cols(
        xp, g_mix[2], wc_kv_t, bias_cols, gain_cols, has_bias=(True, False), normed=(False, False), tm=tm,
        grid=(mp // tm,), x_map=lambda i: (i, 0), out_specs=[flat, flat, blocked, blocked],
        out_shape=[jax.ShapeDtypeStruct((nb_p, D_MODEL, seq), F32)] * 2
        + [jax.ShapeDtypeStruct((mp // SB_BLOCK, D_MODEL, SB_BLOCK), BF16)] * 2)
    xp = _proj_res(xp, _sb_prompt(qp[0], ktb, vtb, nb=nb_p, seq=seq), wc_o, tm=tm)
    qs, ks_new, vs_new = _proj_rows(xs, g_mix[2], bf(w_c_qkv), bias_rows, ones_rows, n_groups=1,
                                    has_bias=(True, True, False), normed=(False,) * 3,
                                    scales=(ATTN_SCALE, 1.0, 1.0), dtypes=(F32,) * 3, tm=tm)
    step_blocks = pl.BlockSpec((ns, D_MODEL, LANES), lambda i: (0, 0, 0))
    sb_kt_s, sb_vt_s = _proj_cols(
        to_sb(xs), g_mix[2], wc_kv_t, bias_cols, gain_cols, has_bias=(True, False), normed=(False, False), tm=tm,
        grid=(1,), x_map=lambda i: (0, 0), out_specs=[step_blocks] * 2,
        out_shape=[jax.ShapeDtypeStruct((ns, D_MODEL, nb_s), F32)] * 2)
    tok = lambda a: a.reshape(nb_s, ns, D_MODEL)
    o_s = _sb_sample(tok(qs[0]), tok(ks_new[0]), tok(vs_new[0]), _positions_minor(cache_sb_k),
                     _positions_minor(cache_sb_v), page_table, nb=nb_s, ns=ns)
    xs = _proj_res(xs, o_s.reshape(ms, D_MODEL), wc_o, tm=tm)
    xp, xs = channel(xp, xs, 2)

    wd_qkv, wd_o = bf(w_d_qkv), bf(w_d_o)
    wd = w_d_qkv.reshape(D_MODEL, N_GROUPS_D, 3, D_MODEL)
    tile_heads = lambda g_: jnp.tile(g_, (1, N_HEADS))
    gain_rows = jnp.stack([tile_heads(g_d_q), tile_heads(g_d_k), jnp.ones((N_GROUPS_D, D_MODEL), F32)],
                          axis=1).reshape(N_GROUPS_D, 3, 1, D_MODEL)
    zero_rows = jnp.zeros((N_GROUPS_D, 3, 1, D_MODEL), F32)
    rows_args = dict(n_groups=N_GROUPS_D, has_bias=(False,) * 3, normed=(True, True, False),
                     scales=(ATTN_SCALE, 1.0, 1.0), tm=tm)
    qkv_p = _proj_rows_dsw(xp, g_mix[3], wd_qkv, gain_rows[:, :2], nb=nb_p, seq=seq, tm=tm)
    outs, lses = zip(*[_dsw_prompt(*qkv_p[g], dil, nb=nb_p, seq=seq) for g, (_, dil) in enumerate(DSW_PATTERNS)])
    xp_attn = _merge_proj(xp, outs, lses, wd_o, seq=seq, tm=tm)
    qs, ks_new, vs_new = _proj_rows(xs, g_mix[3], wd_qkv, zero_rows, gain_rows, dtypes=(F32,) * 3, **rows_args)
    xs_sb = to_sb(xs)
    rows_p, rows_s = [], []
    for g, (win, _) in enumerate(DSW_PATTERNS):
        wt = bf(wd[:, g, 1:].transpose(1, 2, 0))
        gain = jnp.stack([jnp.broadcast_to(g_d_k[g][:, None], (HEAD_DIM, LANES)), jnp.ones((HEAD_DIM, LANES), F32)])
        no_bias = jnp.zeros((2, D_MODEL, LANES), F32)
        keep = min(win, seq)
        tw = min(tm, keep)
        first = (seq - keep) // tw
        kt, vt = _proj_cols(
            xp, g_mix[3], wt, no_bias, gain, has_bias=(False, False), normed=(True, False), tm=tw,
            grid=(nb_p, keep // tw), x_map=lambda b, j, first=first, per=seq // tw: (b * per + first + j, 0),
            out_specs=[pl.BlockSpec((None, D_MODEL, tw), lambda b, j: (b, 0, j))] * 2,
            out_shape=[jax.ShapeDtypeStruct((nb_p, D_MODEL, keep), F32)] * 2)
        rows_p += [_positions_major(kt), _positions_major(vt)]
        kt, vt = _proj_cols(
            xs_sb, g_mix[3], wt, no_bias, gain, has_bias=(False, False), normed=(True, False), tm=tm,
            grid=(1,), x_map=lambda i: (0, 0), out_specs=[step_blocks] * 2,
            out_shape=[jax.ShapeDtypeStruct((ns, D_MODEL, nb_s), F32)] * 2)
        rows_s += [_steps_minor(kt, nb_s), _steps_minor(vt, nb_s)]
    tok3 = lambda a: a.reshape(N_GROUPS_D, nb_s, ns, D_MODEL)
    buffers_t = [_positions_minor(b) for b in
                 (cache_dsw0_k, cache_dsw0_v, cache_dsw1_k, cache_dsw1_v, cache_dsw2_k, cache_dsw2_v)]
    o_s = _dsw_sample(tok3(qs), tok3(ks_new), tok3(vs_new), buffers_t, nb=nb_s, ns=ns)
    xs = _proj_res(xs, o_s.reshape(ms, D_MODEL), wd_o, tm=tm)
    xp, xs = channel(xp_attn, xs, 3)

    return (xp.reshape(nb_p, seq, D_MODEL), xs.reshape(nb_s, ns, D_MODEL), chunk_v_p, chunk_v_s,
            conv_p, conv_s,
            _positions_major(sb_kt_p), _positions_major(sb_vt_p),
            _steps_minor(sb_kt_s, nb_s), _steps_minor(sb_vt_s, nb_s),
            *rows_p, *rows_s)
```
